```python
import jax
import jax.numpy as jnp
from jax import lax
import numpy as np

D_MODEL = 1024
BATCH = 32
SEQ = 2048
DEPTH = 4
DEC_BATCH = 8
DEC_SEQ = 64
PAST_LEN = 2048

CHUNK = 64
D_MIX = D_MODEL
CONV_DIM = D_MIX // 2
CONV_W = 3
GLA_HEADS = 4
GLA_DV = D_MIX - CONV_DIM
GLA_HEAD_V = GLA_DV // GLA_HEADS
GLA_HEAD_K = GLA_HEAD_V // 2
GLA_QK = GLA_HEADS * GLA_HEAD_K
GATE_RANK = 16
GATE_TEMP = 16.0
D_FF = 3584
N_EXPERTS = 8
TOP_K = 2
N_DENSE = (DEPTH + 1) // 2
N_MOE = DEPTH // 2
ALPHA = (2 * DEPTH) ** 0.25
BETA = (8 * DEPTH) ** -0.25
LN_EPS = 1e-5
IN_COLS = 3 * CONV_DIM + 2 * GLA_QK + 2 * GLA_DV + GATE_RANK
IN_SPLITS = (CONV_DIM, 2 * CONV_DIM, 3 * CONV_DIM, 3 * CONV_DIM + GLA_QK, 3 * CONV_DIM + 2 * GLA_QK,
             3 * CONV_DIM + 2 * GLA_QK + GLA_DV, 3 * CONV_DIM + 2 * GLA_QK + 2 * GLA_DV)

kernel_name = "hymba_conv_gla_deepnorm_stream_step"


def _layer_norm(x, g, b):
    xf = x.astype(jnp.float32)
    mu = jnp.mean(xf, axis=-1, keepdims=True)
    var = jnp.mean(jnp.square(xf - mu), axis=-1, keepdims=True)
    y = (xf - mu) * lax.rsqrt(var + LN_EPS) * g.astype(jnp.float32) + b.astype(jnp.float32)
    return y.astype(x.dtype)


def _short_conv(u, buf, w):
    T = u.shape[1]
    u_ext = jnp.concatenate([buf.astype(u.dtype), u], axis=1)
    out = u_ext[:, 0:T] * w[0]
    for i in range(1, CONV_W):
        out = out + u_ext[:, i:i + T] * w[i]
    return out, u_ext[:, T:]


def _gla_chunked(q, k, v, log_a, s0):
    B, T, H, DK = q.shape
    DV = v.shape[-1]
    L = min(CHUNK, T)
    NC = T // L
    f32 = jnp.float32
    q = q.astype(f32).reshape(B, NC, L, H, DK)
    k = k.astype(f32).reshape(B, NC, L, H, DK)
    v = v.astype(f32).reshape(B, NC, L, H, DV)
    b = jnp.cumsum(log_a.astype(f32).reshape(B, NC, L, H, DK), axis=2)
    b_last = b[:, :, -1:]
    qg = q * jnp.exp(b)
    kg = k * jnp.exp(-b)
    kd = k * jnp.exp(b_last - b)
    causal = jnp.tril(jnp.ones((L, L), dtype=bool))
    scores = jnp.einsum('bnihk,bnjhk->bnhij', qg, kg)
    scores = jnp.where(causal, scores, 0.0)
    o_intra = jnp.einsum('bnhij,bnjhv->bnihv', scores, v)
    d_state = jnp.einsum('bnjhk,bnjhv->bnhkv', kd, v)
    decay = jnp.exp(b_last[:, :, 0])

    def step(s, inp):
        dec, ds = inp
        return dec[..., None] * s + ds, s

    s_final, s_prev = lax.scan(step, s0.astype(f32),
                               (jnp.moveaxis(decay, 1, 0), jnp.moveaxis(d_state, 1, 0)))
    s_prev = jnp.moveaxis(s_prev, 0, 1)
    o_inter = jnp.einsum('bnihk,bnhkv->bnihv', qg, s_prev)
    o = (o_intra + o_inter).reshape(B, T, H, DV)
    return o, s_final


def _mixer(x, conv_buf, gla_s, w_in, w_alpha2, b_alpha, conv_w, gla_norm_g, w_out):
    B, T, _ = x.shape
    proj = x @ w_in
    cb, cc, ch, q, k, v, g, a_lr = jnp.split(proj, IN_SPLITS, axis=-1)
    conv_out, new_buf = _short_conv(cc * ch, conv_buf, conv_w)
    y_conv = cb * conv_out
    log_a = jax.nn.log_sigmoid((a_lr @ w_alpha2 + b_alpha).astype(jnp.float32)) / GATE_TEMP
    qh = q.reshape(B, T, GLA_HEADS, GLA_HEAD_K) * (GLA_HEAD_K ** -0.5)
    kh = k.reshape(B, T, GLA_HEADS, GLA_HEAD_K)
    vh = v.reshape(B, T, GLA_HEADS, GLA_HEAD_V)
    ah = log_a.reshape(B, T, GLA_HEADS, GLA_HEAD_K)
    o, new_s = _gla_chunked(qh, kh, vh, ah, gla_s)
    o = o * lax.rsqrt(jnp.mean(jnp.square(o), axis=-1, keepdims=True) + LN_EPS) * gla_norm_g.astype(jnp.float32)
    y_gla = o.reshape(B, T, GLA_DV).astype(x.dtype) * jax.nn.silu(g)
    out = jnp.concatenate([y_conv, y_gla], axis=-1) @ w_out
    return out, new_buf, new_s


def _swiglu(x, w_gate, w_up, w_down):
    return (jax.nn.silu(x @ w_gate) * (x @ w_up)) @ w_down


def _moe(x, router, w_gate, w_up, w_down):
    B, T, D = x.shape
    xf = x.reshape(B * T, D)
    logits = (xf @ router).astype(jnp.float32)
    top_v, top_i = lax.top_k(logits, TOP_K)
    gates = jax.nn.softmax(top_v, axis=-1)
    combine = jnp.sum(jax.nn.one_hot(top_i, N_EXPERTS, dtype=jnp.float32) * gates[..., None], axis=1)
    combine = combine.astype(x.dtype)
    y = jnp.zeros_like(xf)
    for e in range(N_EXPERTS):
        y = y + combine[:, e:e + 1] * _swiglu(xf, w_gate[e], w_up[e], w_down[e])
    return y.reshape(B, T, D)


def _trunk(x, conv_bufs, gla_states, ln_mix_g, ln_mix_b, w_in, w_alpha2, b_alpha, conv_w, gla_norm_g,
           w_out, ln_ffn_g, ln_ffn_b, ffn_w_gate, ffn_w_up, ffn_w_down, moe_router, moe_w_gate,
           moe_w_up, moe_w_down):
    new_states = []
    new_bufs = []
    for l in range(DEPTH):
        m, nb, ns = _mixer(x, conv_bufs[l], gla_states[l], w_in[l], w_alpha2[l], b_alpha[l], conv_w[l],
                           gla_norm_g[l], w_out[l])
        x = _layer_norm(ALPHA * x + m, ln_mix_g[l], ln_mix_b[l])
        if l % 2 == 0:
            f = _swiglu(x, ffn_w_gate[l // 2], ffn_w_up[l // 2], ffn_w_down[l // 2])
        else:
            f = _moe(x, moe_router[l // 2], moe_w_gate[l // 2], moe_w_up[l // 2], moe_w_down[l // 2])
        x = _layer_norm(ALPHA * x + f, ln_ffn_g[l], ln_ffn_b[l])
        new_states.append(ns)
        new_bufs.append(nb)
    return x, jnp.stack(new_states), jnp.stack(new_bufs)


def setup_inputs(seed: int = 0) -> dict:
    key = jax.random.key(seed)
    ks = jax.random.split(key, 24)
    f32 = jnp.float32

    def nrm(k, shape, scale):
        return jax.random.normal(k, shape, f32) * scale

    return {
        "x_prompt": nrm(ks[0], (BATCH, SEQ, D_MODEL), 1.0),
        "x_sample": nrm(ks[1], (DEC_BATCH, DEC_SEQ, D_MODEL), 1.0),
        "cache_conv": nrm(ks[2], (DEPTH, DEC_BATCH, CONV_W - 1, CONV_DIM), 1.0),
        "state_gla": nrm(ks[3], (DEPTH, DEC_BATCH, GLA_HEADS, GLA_HEAD_K, GLA_HEAD_V), 0.5),
        "ln_mix_g": 1.0 + nrm(ks[4], (DEPTH, D_MODEL), 0.02),
        "ln_mix_b": nrm(ks[5], (DEPTH, D_MODEL), 0.02),
        "w_in": nrm(ks[6], (DEPTH, D_MODEL, IN_COLS), D_MODEL ** -0.5),
        "w_alpha2": nrm(ks[7], (DEPTH, GATE_RANK, GLA_QK), GATE_RANK ** -0.5),
        "b_alpha": nrm(ks[8], (DEPTH, GLA_QK), 0.1),
        "conv_w": nrm(ks[9], (DEPTH, CONV_W, CONV_DIM), CONV_W ** -0.5),
        "gla_norm_g": 1.0 + nrm(ks[10], (DEPTH, GLA_HEAD_V), 0.02),
        "w_out": nrm(ks[11], (DEPTH, D_MIX, D_MODEL), BETA * D_MIX ** -0.5),
        "ln_ffn_g": 1.0 + nrm(ks[12], (DEPTH, D_MODEL), 0.02),
        "ln_ffn_b": nrm(ks[13], (DEPTH, D_MODEL), 0.02),
        "ffn_w_gate": nrm(ks[14], (N_DENSE, D_MODEL, D_FF), D_MODEL ** -0.5),
        "ffn_w_up": nrm(ks[15], (N_DENSE, D_MODEL, D_FF), D_MODEL ** -0.5),
        "ffn_w_down": nrm(ks[16], (N_DENSE, D_FF, D_MODEL), BETA * D_FF ** -0.5),
        "moe_router": nrm(ks[17], (N_MOE, D_MODEL, N_EXPERTS), D_MODEL ** -0.5),
        "moe_w_gate": nrm(ks[18], (N_MOE, N_EXPERTS, D_MODEL, D_FF), D_MODEL ** -0.5),
        "moe_w_up": nrm(ks[19], (N_MOE, N_EXPERTS, D_MODEL, D_FF), D_MODEL ** -0.5),
        "moe_w_down": nrm(ks[20], (N_MOE, N_EXPERTS, D_FF, D_MODEL), BETA * D_FF ** -0.5),
    }


def reference(x_prompt, x_sample, cache_conv, state_gla, ln_mix_g, ln_mix_b, w_in, w_alpha2, b_alpha,
              conv_w, gla_norm_g, w_out, ln_ffn_g, ln_ffn_b, ffn_w_gate, ffn_w_up, ffn_w_down,
              moe_router, moe_w_gate, moe_w_up, moe_w_down):
    weights = (ln_mix_g, ln_mix_b, w_in, w_alpha2, b_alpha, conv_w, gla_norm_g, w_out, ln_ffn_g,
               ln_ffn_b, ffn_w_gate, ffn_w_up, ffn_w_down, moe_router, moe_w_gate, moe_w_up, moe_w_down)
    n_prompt = x_prompt.shape[0]
    conv0 = jnp.zeros((DEPTH, n_prompt, CONV_W - 1, CONV_DIM), x_prompt.dtype)
    gla0 = jnp.zeros((DEPTH, n_prompt, GLA_HEADS, GLA_HEAD_K, GLA_HEAD_V), jnp.float32)
    y_prompt, state_gla_prompt, cache_conv_prompt = _trunk(x_prompt, conv0, gla0, *weights)
    y_sample, state_gla_sample, cache_conv_sample = _trunk(x_sample, cache_conv, state_gla, *weights)
    return (y_prompt, y_sample, state_gla_prompt, cache_conv_prompt, state_gla_sample, cache_conv_sample)
```

```python
import functools

import jax
import jax.numpy as jnp
from jax import lax
from jax.experimental import pallas as pl
from jax.experimental.pallas import tpu as pltpu

F32 = jnp.float32
BF16 = jnp.bfloat16

D_MODEL = 1024
DEPTH = 4
CHUNK = 64
CONV_DIM = 512
CONV_W = 3
GLA_HEADS = 4
GLA_HEAD_K = 64
GLA_HEAD_V = 128
GLA_QK = GLA_HEADS * GLA_HEAD_K
GLA_DV = GLA_HEADS * GLA_HEAD_V
GATE_RANK = 16
GATE_TEMP = 16.0
D_FF = 3584
N_EXPERTS = 8
ALPHA = (2 * DEPTH) ** 0.25
LN_EPS = 1e-5

LANES = 128
SUBLANES = 8
VMEM_LIMIT_BYTES = 56 * 1024 * 1024

COL_CB, COL_CC, COL_CH = 0, CONV_DIM, 2 * CONV_DIM
COL_Q = 3 * CONV_DIM
COL_K = COL_Q + GLA_QK
COL_V = COL_K + GLA_QK
COL_G = COL_V + GLA_DV
COL_A = COL_G + GLA_DV
IN_COLS_PAD = COL_A + LANES

TB = 512
SUB = 256
CHUNKS_PER_BLOCK = TB // CHUNK
TF = 512
TM_MOE = 1024
TM_DENSE = 1024
LANE_I1, LANE_I2, LANE_G1, LANE_G2 = 8, 9, 10, 11


def _dot(a, b):
    return jnp.dot(a, b, preferred_element_type=F32)


def _dot_tn(a, b):
    return lax.dot_general(a, b, (((0,), (0,)), ((), ())), preferred_element_type=F32)


def _dot_nt(a, b):
    return lax.dot_general(a, b, (((1,), (1,)), ((), ())), preferred_element_type=F32)


def _split_bf16(a):
    hi = a.astype(BF16)
    lo = (a - hi.astype(F32)).astype(BF16)
    return hi, lo


def _split_bf16_wrapper(a):
    hi = lax.reduce_precision(a, exponent_bits=8, mantissa_bits=7)
    return hi.astype(BF16), (a - hi).astype(BF16)


def _parts(a, precise):
    return _split_bf16(a) if precise else (a.astype(BF16),)


def _mm(a, b, dot=_dot):
    r = dot(a[0], b[0])
    if len(a) > 1:
        r = r + dot(a[1], b[0])
    if len(b) > 1:
        r = r + dot(a[0], b[1])
    return r


def _load(refs, idx=None):
    return tuple(r[...] if idx is None else r[idx] for r in refs)


def _store(refs, idx, parts):
    for r, p in zip(refs, parts):
        r[idx] = p


def _layer_norm(r, g, b):
    mu = jnp.mean(r, axis=-1, keepdims=True)
    d = r - mu
    var = jnp.mean(d * d, axis=-1, keepdims=True)
    return d * lax.rsqrt(var + LN_EPS) * g + b


def _silu(g):
    return g / (1.0 + jnp.exp(-g))


def _mixer_kernel(*refs, sample, precise, blocks_per_seq, with_router, n_alias):
    np_ = 2 if precise else 1
    it = iter(refs)

    def take(n):
        return [next(it) for _ in range(n)]

    x_ref = next(it)
    if sample:
        convin_ref = next(it)
        statein_ref = next(it)
    w_in_refs = take(np_)
    wa2_refs = take(np_)
    balpha_ref = next(it)
    convw_ref = next(it)
    gnorm_ref = next(it)
    w_out_refs = take(np_)
    lng_ref = next(it)
    lnb_ref = next(it)
    if with_router:
        rhi_ref = next(it)
        rlo_ref = next(it)
    take(n_alias)
    x1_ref = next(it)
    if with_router:
        x1b_ref = next(it)
        route_ref = next(it)
    convout_ref = next(it)
    stateout_ref = next(it)
    proj_ref = next(it)
    ue_ref = next(it)
    prev_ref = next(it)
    fx_ref = next(it)
    y_ref = next(it)
    o_ref = next(it)
    qg_refs = take(np_)
    kd_refs = take(np_)
    v_refs = take(np_)
    la_refs = take(2)
    s_ref = next(it)

    i = pl.program_id(0)
    seq_start = (i % blocks_per_seq) == 0

    proj_ref[...] = _mm(_parts(x_ref[...], precise), _load(w_in_refs))

    if sample:
        prev_ref[6:8, :] = convin_ref[0]
    else:
        @pl.when(seq_start)
        def _():
            prev_ref[...] = jnp.zeros_like(prev_ref)

    w0 = convw_ref[0:1, :]
    w1 = convw_ref[1:2, :]
    w2 = convw_ref[2:3, :]
    u = proj_ref[:, COL_CC:COL_CC + CONV_DIM] * proj_ref[:, COL_CH:COL_CH + CONV_DIM]
    ue_ref[0:8, :] = prev_ref[...]
    ue_ref[8:8 + TB, :] = u
    conv = ue_ref[6:6 + TB, :] * w0 + ue_ref[7:7 + TB, :] * w1 + u * w2
    y_ref[:, 0:CONV_DIM] = proj_ref[:, COL_CB:COL_CB + CONV_DIM] * conv
    if sample:
        for c in range(CHUNKS_PER_BLOCK):
            r0 = c * CHUNK
            if c > 0:
                fx_ref[c, 6:8, :] = convin_ref[c]
                fx_ref[c, 8:16, :] = ue_ref[8 + r0:16 + r0, :]
                convc = fx_ref[c, 6:14, :] * w0 + fx_ref[c, 7:15, :] * w1 + fx_ref[c, 8:16, :] * w2
                y_ref[r0:r0 + 8, 0:CONV_DIM] = proj_ref[r0:r0 + 8, COL_CB:COL_CB + CONV_DIM] * convc
            convout_ref[c] = ue_ref[8 + r0 + CHUNK - 2:8 + r0 + CHUNK, :]
    else:
        prev_ref[...] = ue_ref[TB:TB + 8, :]
        convout_ref[0] = ue_ref[TB + 6:TB + 8, :]

    z = _mm(_parts(proj_ref[:, COL_A:COL_A + LANES], precise), _load(wa2_refs)) + balpha_ref[...]
    la = (jnp.minimum(z, 0.0) - jnp.log(1.0 + jnp.exp(-jnp.abs(z)))) * (1.0 / GATE_TEMP)
    _store(la_refs, slice(None), _split_bf16(la))

    row = lax.broadcasted_iota(jnp.int32, (SUB, SUB), 0)
    col = lax.broadcasted_iota(jnp.int32, (SUB, SUB), 1)
    same_chunk = (row // CHUNK) == (col // CHUNK)
    causal = jnp.logical_and(same_chunk, row >= col)
    tril_bd = (causal.astype(BF16),)
    ones_bd = (same_chunk.astype(BF16),)
    lane_head = lax.broadcasted_iota(jnp.int32, (SUB, GLA_QK), 1) // GLA_HEAD_K

    for sb in range(TB // SUB):
        rs = slice(sb * SUB, (sb + 1) * SUB)
        lap = _load(la_refs, (rs, slice(None)))
        bcum = _mm(tril_bd, lap)
        blast = _mm(ones_bd, lap)
        q = proj_ref[rs, COL_Q:COL_Q + GLA_QK]
        k = proj_ref[rs, COL_K:COL_K + GLA_QK]
        qg = q * jnp.exp(bcum) * (GLA_HEAD_K ** -0.5)
        kg = _parts(k * jnp.exp(-bcum), precise)
        _store(qg_refs, (rs, slice(None)), _parts(qg, precise))
        _store(kd_refs, (rs, slice(None)), _parts(k * jnp.exp(blast - bcum), precise))
        vp = _parts(proj_ref[rs, COL_V:COL_V + GLA_DV], precise)
        _store(v_refs, (rs, slice(None)), vp)
        for h in range(GLA_HEADS):
            vs = slice(h * GLA_HEAD_V, (h + 1) * GLA_HEAD_V)
            qm = _parts(jnp.where(lane_head == h, qg, 0.0), precise)
            sc = jnp.where(causal, _mm(qm, kg, _dot_nt), 0.0)
            o_ref[rs, vs] = _mm(_parts(sc, precise), tuple(p[:, vs] for p in vp))

    ones_cv = (jnp.ones((CHUNK, GLA_HEAD_V), BF16),)
    bd_row = lax.broadcasted_iota(jnp.int32, (GLA_QK, GLA_DV), 0) // GLA_HEAD_K
    bd_col = lax.broadcasted_iota(jnp.int32, (GLA_QK, GLA_DV), 1) // GLA_HEAD_V
    head_diag = bd_row == bd_col
    if not sample:
        s = jnp.where(seq_start, 0.0, s_ref[...])
    for c in range(CHUNKS_PER_BLOCK):
        cs = (slice(c * CHUNK, (c + 1) * CHUNK), slice(None))
        if sample:
            s = statein_ref[c]
        s_rows = s.reshape(GLA_QK, GLA_HEAD_V)
        s_bd = jnp.where(head_diag, jnp.concatenate([s_rows] * GLA_HEADS, axis=1), 0.0)
        o_ref[cs] += _mm(_load(qg_refs, cs), _parts(s_bd, precise))
        ds = _mm(_load(kd_refs, cs), _load(v_refs, cs), _dot_tn)
        dec = jnp.exp(_mm(_load(la_refs, cs), ones_cv, _dot_tn))
        ds_diag = jnp.stack([ds[h * GLA_HEAD_K:(h + 1) * GLA_HEAD_K, h * GLA_HEAD_V:(h + 1) * GLA_HEAD_V]
                             for h in range(GLA_HEADS)])
        s = dec.reshape(GLA_HEADS, GLA_HEAD_K, GLA_HEAD_V) * s + ds_diag
        if sample:
            stateout_ref[c] = s
    if not sample:
        s_ref[...] = s
        stateout_ref[0] = s

    gn = gnorm_ref[...]
    for h in range(GLA_HEADS):
        vs = slice(h * GLA_HEAD_V, (h + 1) * GLA_HEAD_V)
        oh = o_ref[:, vs]
        ms = jnp.mean(oh * oh, axis=-1, keepdims=True)
        g = proj_ref[:, COL_G + h * GLA_HEAD_V:COL_G + (h + 1) * GLA_HEAD_V]
        y_ref[:, CONV_DIM + h * GLA_HEAD_V:CONV_DIM + (h + 1) * GLA_HEAD_V] = (
            oh * lax.rsqrt(ms + LN_EPS) * gn * _silu(g))

    m = _mm(_parts(y_ref[...], precise), _load(w_out_refs))
    x1 = _layer_norm(ALPHA * x_ref[...] + m, lng_ref[...], lnb_ref[...])
    x1_ref[...] = x1

    if with_router:
        x1b_ref[...] = x1.astype(BF16)
        lg = _mm(_split_bf16(x1), (rhi_ref[...], rlo_ref[...]))
        lane = lax.broadcasted_iota(jnp.int32, lg.shape, 1)
        neg = jnp.float32(-jnp.inf)
        lg = jnp.where(lane < N_EXPERTS, lg, neg)
        m1 = jnp.max(lg, axis=-1, keepdims=True)
        i1 = jnp.min(jnp.where(lg == m1, lane, LANES), axis=-1, keepdims=True)
        lg2 = jnp.where(lane == i1, neg, lg)
        m2 = jnp.max(lg2, axis=-1, keepdims=True)
        i2 = jnp.min(jnp.where(lg2 == m2, lane, LANES), axis=-1, keepdims=True)
        e2 = jnp.exp(m2 - m1)
        g1 = 1.0 / (1.0 + e2)
        g2 = e2 / (1.0 + e2)
        slab = jnp.where(lane == i1, g1, jnp.where(lane == i2, g2, 0.0))
        slab = jnp.where(lane == LANE_I1, i1.astype(F32), slab)
        slab = jnp.where(lane == LANE_I2, i2.astype(F32), slab)
        slab = jnp.where(lane == LANE_G1, g1, slab)
        slab = jnp.where(lane == LANE_G2, g2, slab)
        route_ref[...] = slab


def _mixer_call(x, w, *, n_rows, n_blocks, blocks_per_seq, with_router, sample_state=None, into=None):
    sample = sample_state is not None
    precise = sample
    np_ = 2 if precise else 1
    const2 = lambda i: (0, 0)
    const3 = lambda i: (0, 0, 0)
    const4 = lambda i: (0, 0, 0, 0)

    x_last = x.shape[0] // TB - 1
    in_specs = [pl.BlockSpec((TB, D_MODEL), (lambda i: (x_last, 0)) if sample else (lambda i: (i, 0)))]
    args = [x]
    if sample:
        conv_in, state_in = sample_state
        in_specs += [pl.BlockSpec(conv_in.shape, const3), pl.BlockSpec(state_in.shape, const4)]
        args += [conv_in, state_in]
    in_specs += [pl.BlockSpec((D_MODEL, IN_COLS_PAD), const2)] * np_
    args += list(w["w_in"][:np_])
    in_specs += [pl.BlockSpec((LANES, GLA_QK), const2)] * np_
    args += list(w["wa2"][:np_])
    in_specs += [pl.BlockSpec((1, GLA_QK), const2), pl.BlockSpec((CONV_W, CONV_DIM), const2),
                 pl.BlockSpec((1, GLA_HEAD_V), const2)]
    args += [w["b_alpha"], w["conv_w"], w["gla_norm_g"]]
    in_specs += [pl.BlockSpec((D_MODEL, D_MODEL), const2)] * np_
    args += list(w["w_out"][:np_])
    in_specs += [pl.BlockSpec((1, D_MODEL), const2)] * 2
    args += [w["ln_g"], w["ln_b"]]
    if with_router:
        in_specs += [pl.BlockSpec((D_MODEL, LANES), const2)] * 2
        args += list(w["router"])
    aliases = {}
    n_alias = 0
    if sample:
        n_alias = len(into)
        for k, buf in enumerate(into):
            aliases[len(args)] = k
            in_specs.append(pl.BlockSpec(memory_space=pl.ANY))
            args.append(buf)

    out_row = (lambda i: (n_blocks, 0)) if sample else (lambda i: (i, 0))
    out_shape = [jax.ShapeDtypeStruct((n_rows, D_MODEL), F32)]
    out_specs = [pl.BlockSpec((TB, D_MODEL), out_row)]
    if with_router:
        out_shape += [jax.ShapeDtypeStruct((n_rows, D_MODEL), BF16),
                      jax.ShapeDtypeStruct((n_rows, LANES), F32)]
        out_specs += [pl.BlockSpec((TB, D_MODEL), out_row), pl.BlockSpec((TB, LANES), out_row)]
    if sample:
        out_shape += [jax.ShapeDtypeStruct(conv_in.shape, F32), jax.ShapeDtypeStruct(state_in.shape, F32)]
        out_specs += [pl.BlockSpec(conv_in.shape, const3), pl.BlockSpec(state_in.shape, const4)]
    else:
        n_seq = n_blocks // blocks_per_seq
        out_shape += [jax.ShapeDtypeStruct((n_seq, CONV_W - 1, CONV_DIM), F32),
                      jax.ShapeDtypeStruct((n_seq, GLA_HEADS, GLA_HEAD_K, GLA_HEAD_V), F32)]
        out_specs += [pl.BlockSpec((1, CONV_W - 1, CONV_DIM), lambda i: (i // blocks_per_seq, 0, 0)),
                      pl.BlockSpec((1, GLA_HEADS, GLA_HEAD_K, GLA_HEAD_V),
                                   lambda i: (i // blocks_per_seq, 0, 0, 0))]
    scratch = [
        pltpu.VMEM((TB, IN_COLS_PAD), F32),
        pltpu.VMEM((TB + 8, CONV_DIM), F32),
        pltpu.VMEM((8, CONV_DIM), F32),
        pltpu.VMEM((CHUNKS_PER_BLOCK, 16, CONV_DIM), F32),
        pltpu.VMEM((TB, D_MODEL), F32),
        pltpu.VMEM((TB, GLA_DV), F32),
    ]
    scratch += [pltpu.VMEM((TB, GLA_QK), BF16)] * np_
    scratch += [pltpu.VMEM((TB, GLA_QK), BF16)] * np_
    scratch += [pltpu.VMEM((TB, GLA_DV), BF16)] * np_
    scratch += [pltpu.VMEM((TB, GLA_QK), BF16)] * 2
    scratch += [pltpu.VMEM((GLA_HEADS, GLA_HEAD_K, GLA_HEAD_V), F32)]
    kern = functools.partial(_mixer_kernel, sample=sample, precise=precise, blocks_per_seq=blocks_per_seq,
                             with_router=with_router, n_alias=n_alias)
    name = ("mixer_sample" if sample else "mixer_prompt") + ("_router" if with_router else "")
    return pl.pallas_call(
        kern,
        grid=(1 if sample else n_blocks,),
        in_specs=in_specs,
        out_specs=out_specs,
        out_shape=out_shape,
        scratch_shapes=scratch,
        input_output_aliases=aliases,
        compiler_params=pltpu.CompilerParams(dimension_semantics=("arbitrary",),
                                             vmem_limit_bytes=VMEM_LIMIT_BYTES),
        name=name,
    )(*args)


def _swiglu_step(xp, wg, wu, wd, acc_ref, precise):
    f = pl.program_id(1)
    g = _mm(xp, wg)
    u = _mm(xp, wu)
    part = _mm(_parts(_silu(g) * u, precise), wd)

    @pl.when(f == 0)
    def _():
        acc_ref[...] = part

    @pl.when(f > 0)
    def _():
        acc_ref[...] += part


def _ffn_dense_kernel(*refs, precise, n_alias):
    np_ = 2 if precise else 1
    x_ref = refs[0]
    wg_refs = refs[1:1 + np_]
    wu_refs = refs[1 + np_:1 + 2 * np_]
    wd_refs = refs[1 + 2 * np_:1 + 3 * np_]
    lng_ref, lnb_ref = refs[1 + 3 * np_:3 + 3 * np_]
    out_ref, acc_ref = refs[3 + 3 * np_ + n_alias:]
    x = x_ref[...]
    _swiglu_step(_parts(x, precise), _load(wg_refs, 0), _load(wu_refs, 0), _load(wd_refs, 0), acc_ref, precise)

    @pl.when(pl.program_id(1) == pl.num_programs(1) - 1)
    def _():
        out_ref[...] = _layer_norm(ALPHA * x + acc_ref[...], lng_ref[...], lnb_ref[...])


def _ffn_dense_call(x, wg, wu, wd, ln_g, ln_b, *, tm, first_block, n_blocks, precise, into=None):
    n_rows = x.shape[0]
    np_ = 2 if precise else 1
    nf = D_FF // TF
    row_spec = pl.BlockSpec((tm, D_MODEL), lambda t, f: (first_block + t, 0))
    in_specs = [row_spec]
    in_specs += [pl.BlockSpec((1, D_MODEL, TF), lambda t, f: (0, 0, f))] * (2 * np_)
    in_specs += [pl.BlockSpec((1, TF, D_MODEL), lambda t, f: (0, f, 0))] * np_
    in_specs += [pl.BlockSpec((1, D_MODEL), lambda t, f: (0, 0))] * 2
    args = [x] + list(wg[:np_]) + list(wu[:np_]) + list(wd[:np_]) + [ln_g, ln_b]
    aliases = {}
    n_alias = 0
    if into is not None:
        n_alias = 1
        aliases[len(args)] = 0
        in_specs.append(pl.BlockSpec(memory_space=pl.ANY))
        args.append(into)
    return pl.pallas_call(
        functools.partial(_ffn_dense_kernel, precise=precise, n_alias=n_alias),
        grid=(n_blocks, nf),
        in_specs=in_specs,
        out_specs=row_spec,
        out_shape=jax.ShapeDtypeStruct((n_rows, D_MODEL), F32),
        scratch_shapes=[pltpu.VMEM((tm, D_MODEL), F32)],
        input_output_aliases=aliases,
        compiler_params=pltpu.CompilerParams(dimension_semantics=("arbitrary", "arbitrary"),
                                             vmem_limit_bytes=VMEM_LIMIT_BYTES),
        name="ffn_dense_precise" if precise else "ffn_dense",
    )(*args)


def _ffn_moe_kernel(te_ref, nt_ref, x_ref, gate_ref, wg_ref, wu_ref, wd_ref, out_ref, acc_ref):
    t = pl.program_id(0)
    used = t < nt_ref[0]

    @pl.when(used)
    def _():
        _swiglu_step((x_ref[...],), (wg_ref[0],), (wu_ref[0],), (wd_ref[0],), acc_ref, False)

    last = pl.program_id(1) == pl.num_programs(1) - 1

    @pl.when(jnp.logical_and(last, used))
    def _():
        out_ref[...] = acc_ref[...] * gate_ref[...]

    @pl.when(jnp.logical_and(last, jnp.logical_not(used)))
    def _():
        out_ref[...] = jnp.zeros_like(out_ref)


def _ffn_moe_call(tile_expert, n_tiles_used, xs, gate_s, wg, wu, wd):
    n_rows = xs.shape[0]
    nf = D_FF // TF
    grid_spec = pltpu.PrefetchScalarGridSpec(
        num_scalar_prefetch=2,
        grid=(n_rows // TM_MOE, nf),
        in_specs=[
            pl.BlockSpec((TM_MOE, D_MODEL), lambda t, f, te, nt: (t, 0)),
            pl.BlockSpec((TM_MOE, 1), lambda t, f, te, nt: (t, 0)),
            pl.BlockSpec((1, D_MODEL, TF), lambda t, f, te, nt: (te[t], 0, f)),
            pl.BlockSpec((1, D_MODEL, TF), lambda t, f, te, nt: (te[t], 0, f)),
            pl.BlockSpec((1, TF, D_MODEL), lambda t, f, te, nt: (te[t], f, 0)),
        ],
        out_specs=pl.BlockSpec((TM_MOE, D_MODEL), lambda t, f, te, nt: (t, 0)),
        scratch_shapes=[pltpu.VMEM((TM_MOE, D_MODEL), F32)],
    )
    return pl.pallas_call(
        _ffn_moe_kernel,
        grid_spec=grid_spec,
        out_shape=jax.ShapeDtypeStruct((n_rows, D_MODEL), F32),
        compiler_params=pltpu.CompilerParams(dimension_semantics=("arbitrary", "arbitrary"),
                                             vmem_limit_bytes=VMEM_LIMIT_BYTES),
        name="ffn_moe",
    )(tile_expert, n_tiles_used, xs, gate_s, wg, wu, wd)


def _combine_kernel(x_ref, ya_ref, yb_ref, lng_ref, lnb_ref, out_ref):
    r = ALPHA * x_ref[...] + (ya_ref[...] + yb_ref[...])
    out_ref[...] = _layer_norm(r, lng_ref[...], lnb_ref[...])


def _combine_call(x, ya, yb, ln_g, ln_b):
    n_rows = x.shape[0]
    tm = TB
    row_spec = pl.BlockSpec((tm, D_MODEL), lambda t: (t, 0))
    vec_spec = pl.BlockSpec((1, D_MODEL), lambda t: (0, 0))
    return pl.pallas_call(
        _combine_kernel,
        grid=(n_rows // tm,),
        in_specs=[row_spec, row_spec, row_spec, vec_spec, vec_spec],
        out_specs=row_spec,
        out_shape=jax.ShapeDtypeStruct((n_rows, D_MODEL), F32),
        compiler_params=pltpu.CompilerParams(dimension_semantics=("arbitrary",),
                                             vmem_limit_bytes=VMEM_LIMIT_BYTES),
        name="moe_combine",
    )(x, ya, yb, ln_g, ln_b)


def _moe_dispatch(route, n_rows):
    i1 = route[:, LANE_I1].astype(jnp.int32)
    i2 = route[:, LANE_I2].astype(jnp.int32)
    g1 = route[:, LANE_G1]
    g2 = route[:, LANE_G2]
    experts = jnp.arange(N_EXPERTS, dtype=jnp.int32)
    hit = ((i1[:, None] == experts) | (i2[:, None] == experts)).astype(jnp.int32)
    before = jnp.cumsum(hit, axis=0) - hit
    counts = jnp.sum(hit, axis=0)
    padded = ((counts + TM_MOE - 1) // TM_MOE) * TM_MOE
    ends = jnp.cumsum(padded)
    starts = ends - padded
    rank1 = jnp.take_along_axis(before, i1[:, None], axis=1)[:, 0]
    rank2 = jnp.take_along_axis(before, i2[:, None], axis=1)[:, 0]
    p1 = starts[i1] + rank1
    p2 = starts[i2] + rank2
    n_tiles = (2 * n_rows + N_EXPERTS * (TM_MOE - 1) + TM_MOE - 1) // TM_MOE
    n_sorted = n_tiles * TM_MOE
    tok = jnp.arange(n_rows, dtype=jnp.int32)
    row_src = jnp.zeros((n_sorted,), jnp.int32).at[p1].set(tok).at[p2].set(tok)
    gate_s = jnp.zeros((n_sorted,), F32).at[p1].set(g1).at[p2].set(g2)
    tile_start = jnp.arange(n_tiles, dtype=jnp.int32) * TM_MOE
    tile_expert = jnp.minimum(jnp.sum(tile_start[:, None] >= ends[None, :], axis=1), N_EXPERTS - 1)
    n_tiles_used = (ends[-1] // TM_MOE).reshape(1)
    return row_src, gate_s[:, None], p1, p2, tile_expert.astype(jnp.int32), n_tiles_used.astype(jnp.int32)


def _pad_w_in(w_in):
    main = w_in[:, :COL_A]
    alr = jnp.pad(w_in[:, COL_A:], ((0, 0), (0, LANES - GATE_RANK)))
    return jnp.concatenate([main, alr], axis=1)


def kernel(x_prompt, x_sample, cache_conv, state_gla, ln_mix_g, ln_mix_b, w_in, w_alpha2, b_alpha,
           conv_w, gla_norm_g, w_out, ln_ffn_g, ln_ffn_b, ffn_w_gate, ffn_w_up, ffn_w_down,
           moe_router, moe_w_gate, moe_w_up, moe_w_down):
    n_prompt_seq, seq_len, _ = x_prompt.shape
    n_sample, dec_len, _ = x_sample.shape
    assert dec_len == CHUNK and n_sample * dec_len == TB and seq_len % TB == 0
    n_prompt_rows = n_prompt_seq * seq_len
    n_rows = n_prompt_rows + TB
    blocks_per_seq = seq_len // TB
    n_prompt_blocks = n_prompt_rows // TB
    assert n_prompt_rows % TM_DENSE == 0

    xp = x_prompt.reshape(n_prompt_rows, D_MODEL)
    xs = x_sample.reshape(TB, D_MODEL)
    gla_p, conv_p, gla_s, conv_s = [], [], [], []
    for l in range(DEPTH):
        moe_layer = l % 2 == 1
        e = l // 2
        w = {
            "w_in": _split_bf16_wrapper(_pad_w_in(w_in[l])),
            "wa2": _split_bf16_wrapper(jnp.pad(w_alpha2[l], ((0, LANES - GATE_RANK), (0, 0)))),
            "b_alpha": b_alpha[l][None, :],
            "conv_w": conv_w[l],
            "gla_norm_g": gla_norm_g[l][None, :],
            "w_out": _split_bf16_wrapper(w_out[l]),
            "ln_g": ln_mix_g[l][None, :],
            "ln_b": ln_mix_b[l][None, :],
        }
        if moe_layer:
            w["router"] = _split_bf16_wrapper(jnp.pad(moe_router[e], ((0, 0), (0, LANES - N_EXPERTS))))
        common = dict(n_rows=n_rows, n_blocks=n_prompt_blocks, blocks_per_seq=blocks_per_seq,
                      with_router=moe_layer)
        outs = _mixer_call(xp, w, **common)
        n_stream = 3 if moe_layer else 1
        cp, sp = outs[n_stream:]
        outs_s = _mixer_call(xs, w, sample_state=(cache_conv[l], state_gla[l]), into=outs[:n_stream], **common)
        cs, ss = outs_s[n_stream:]
        conv_p.append(cp)
        gla_p.append(sp)
        conv_s.append(cs)
        gla_s.append(ss)

        ln_g = ln_ffn_g[l][None, :]
        ln_b = ln_ffn_b[l][None, :]
        if moe_layer:
            x1, x1b, route = outs_s[:3]
            row_src, gate_s, p1, p2, tile_expert, n_tiles_used = _moe_dispatch(route, n_rows)
            xsort = jnp.take(x1b, row_src, axis=0)
            ys = _ffn_moe_call(tile_expert, n_tiles_used, xsort, gate_s, moe_w_gate[e].astype(BF16),
                               moe_w_up[e].astype(BF16), moe_w_down[e].astype(BF16))
            x2 = _combine_call(x1, jnp.take(ys, p1, axis=0), jnp.take(ys, p2, axis=0), ln_g, ln_b)
        else:
            x1 = outs_s[0]
            precise = l == 0
            split = _split_bf16_wrapper if precise else (lambda a: (a.astype(BF16),))
            wg = split(ffn_w_gate[e:e + 1])
            wu = split(ffn_w_up[e:e + 1])
            wd = split(ffn_w_down[e:e + 1])
            x2 = _ffn_dense_call(x1, wg, wu, wd, ln_g, ln_b, tm=TM_DENSE, first_block=0,
                                 n_blocks=n_prompt_rows // TM_DENSE, precise=False)
            x2 = _ffn_dense_call(x1, wg, wu, wd, ln_g, ln_b, tm=TB, first_block=n_prompt_blocks,
                                 n_blocks=1, precise=precise, into=x2)
        xp = xs = x2

    y_prompt = xp[:n_prompt_rows].reshape(n_prompt_seq, seq_len, D_MODEL)
    y_sample = xp[n_prompt_rows:].reshape(n_sample, dec_len, D_MODEL)
    return (y_prompt, y_sample, jnp.stack(gla_p), jnp.stack(conv_p), jnp.stack(gla_s), jnp.stack(conv_s))
```

```python
import functools

import jax
import jax.numpy as jnp
from jax import lax
from jax.experimental import pallas as pl
from jax.experimental.pallas import tpu as pltpu

F32 = jnp.float32
BF16 = jnp.bfloat16

D_MODEL = 1024
DEPTH = 4
CHUNK = 64
CONV_DIM = 512
CONV_W = 3
GLA_HEADS = 4
GLA_HEAD_K = 64
GLA_HEAD_V = 128
GLA_QK = GLA_HEADS * GLA_HEAD_K
GLA_DV = GLA_HEADS * GLA_HEAD_V
GATE_RANK = 16
GATE_TEMP = 16.0
D_FF = 3584
N_EXPERTS = 8
ALPHA = (2 * DEPTH) ** 0.25
LN_EPS = 1e-5

LANES = 128
SUBLANES = 8
VMEM_LIMIT_BYTES = 56 * 1024 * 1024

COL_CB, COL_CC, COL_CH = 0, CONV_DIM, 2 * CONV_DIM
COL_Q = 3 * CONV_DIM
COL_K = COL_Q + GLA_QK
COL_V = COL_K + GLA_QK
COL_G = COL_V + GLA_DV
COL_A = COL_G + GLA_DV
IN_COLS_PAD = COL_A + LANES

TB = 512
SUB = 256
CHUNKS_PER_BLOCK = TB // CHUNK
TF = 512
TM_MOE = 1024
TM_DENSE = 1024
LANE_I1, LANE_I2, LANE_G1, LANE_G2 = 8, 9, 10, 11


def _dot(a, b):
    return jnp.dot(a, b, preferred_element_type=F32)


def _dot_tn(a, b):
    return lax.dot_general(a, b, (((0,), (0,)), ((), ())), preferred_element_type=F32)


def _dot_nt(a, b):
    return lax.dot_general(a, b, (((1,), (1,)), ((), ())), preferred_element_type=F32)


def _split_bf16(a):
    hi = a.astype(BF16)
    lo = (a - hi.astype(F32)).astype(BF16)
    return hi, lo


def _parts(a, precise):
    return _split_bf16(a) if precise else (a.astype(BF16),)


def _mm(a, b, dot=_dot):
    r = dot(a[0], b[0])
    if len(a) > 1:
        r = r + dot(a[1], b[0])
    if len(b) > 1:
        r = r + dot(a[0], b[1])
    return r


def _load(refs, idx=None):
    return tuple(r[...] if idx is None else r[idx] for r in refs)


def _store(refs, idx, parts):
    for r, p in zip(refs, parts):
        r[idx] = p


def _layer_norm(r, g, b):
    mu = jnp.mean(r, axis=-1, keepdims=True)
    d = r - mu
    var = jnp.mean(d * d, axis=-1, keepdims=True)
    return d * lax.rsqrt(var + LN_EPS) * g + b


def _silu(g):
    return g / (1.0 + jnp.exp(-g))


def _mixer_kernel(*refs, sample, precise, blocks_per_seq, with_router, n_alias):
    np_ = 2 if precise else 1
    it = iter(refs)

    def take(n):
        return [next(it) for _ in range(n)]

    x_ref = next(it)
    if sample:
        convin_ref = next(it)
        statein_ref = next(it)
    w_in_refs = take(np_)
    wa2_refs = take(np_)
    balpha_ref = next(it)
    convw_ref = next(it)
    gnorm_ref = next(it)
    w_out_refs = take(np_)
    lng_ref = next(it)
    lnb_ref = next(it)
    if with_router:
        rhi_ref = next(it)
        rlo_ref = next(it)
    take(n_alias)
    x1_ref = next(it)
    if with_router:
        x1b_ref = next(it)
        route_ref = next(it)
    convout_ref = next(it)
    stateout_ref = next(it)
    proj_ref = next(it)
    ue_ref = next(it)
    prev_ref = next(it)
    fx_ref = next(it)
    y_ref = next(it)
    o_ref = next(it)
    qg_refs = take(np_)
    kd_refs = take(np_)
    v_refs = take(np_)
    la_refs = take(2)
    s_ref = next(it)

    i = pl.program_id(0)
    seq_start = (i % blocks_per_seq) == 0

    proj_ref[...] = _mm(_parts(x_ref[...], precise), _load(w_in_refs))

    if sample:
        prev_ref[6:8, :] = convin_ref[0]
    else:
        @pl.when(seq_start)
        def _():
            prev_ref[...] = jnp.zeros_like(prev_ref)

    w0 = convw_ref[0:1, :]
    w1 = convw_ref[1:2, :]
    w2 = convw_ref[2:3, :]
    u = proj_ref[:, COL_CC:COL_CC + CONV_DIM] * proj_ref[:, COL_CH:COL_CH + CONV_DIM]
    ue_ref[0:8, :] = prev_ref[...]
    ue_ref[8:8 + TB, :] = u
    conv = ue_ref[6:6 + TB, :] * w0 + ue_ref[7:7 + TB, :] * w1 + u * w2
    y_ref[:, 0:CONV_DIM] = proj_ref[:, COL_CB:COL_CB + CONV_DIM] * conv
    if sample:
        for c in range(CHUNKS_PER_BLOCK):
            r0 = c * CHUNK
            if c > 0:
                fx_ref[c, 6:8, :] = convin_ref[c]
                fx_ref[c, 8:16, :] = ue_ref[8 + r0:16 + r0, :]
                convc = fx_ref[c, 6:14, :] * w0 + fx_ref[c, 7:15, :] * w1 + fx_ref[c, 8:16, :] * w2
                y_ref[r0:r0 + 8, 0:CONV_DIM] = proj_ref[r0:r0 + 8, COL_CB:COL_CB + CONV_DIM] * convc
            convout_ref[c] = ue_ref[8 + r0 + CHUNK - 2:8 + r0 + CHUNK, :]
    else:
        prev_ref[...] = ue_ref[TB:TB + 8, :]
        convout_ref[0] = ue_ref[TB + 6:TB + 8, :]

    z = _mm(_parts(proj_ref[:, COL_A:COL_A + LANES], precise), _load(wa2_refs)) + balpha_ref[...]
    la = (jnp.minimum(z, 0.0) - jnp.log(1.0 + jnp.exp(-jnp.abs(z)))) * (1.0 / GATE_TEMP)
    _store(la_refs, slice(None), _split_bf16(la))

    row = lax.broadcasted_iota(jnp.int32, (SUB, SUB), 0)
    col = lax.broadcasted_iota(jnp.int32, (SUB, SUB), 1)
    same_chunk = (row // CHUNK) == (col // CHUNK)
    causal = jnp.logical_and(same_chunk, row >= col)
    tril_bd = (causal.astype(BF16),)
    ones_bd = (same_chunk.astype(BF16),)
    lane_head = lax.broadcasted_iota(jnp.int32, (SUB, GLA_QK), 1) // GLA_HEAD_K

    for sb in range(TB // SUB):
        rs = slice(sb * SUB, (sb + 1) * SUB)
        lap = _load(la_refs, (rs, slice(None)))
        bcum = _mm(tril_bd, lap)
        blast = _mm(ones_bd, lap)
        q = proj_ref[rs, COL_Q:COL_Q + GLA_QK]
        k = proj_ref[rs, COL_K:COL_K + GLA_QK]
        qg = q * jnp.exp(bcum) * (GLA_HEAD_K ** -0.5)
        kg = _parts(k * jnp.exp(-bcum), precise)
        _store(qg_refs, (rs, slice(None)), _parts(qg, precise))
        _store(kd_refs, (rs, slice(None)), _parts(k * jnp.exp(blast - bcum), precise))
        vp = _parts(proj_ref[rs, COL_V:COL_V + GLA_DV], precise)
        _store(v_refs, (rs, slice(None)), vp)
        for h in range(GLA_HEADS):
            vs = slice(h * GLA_HEAD_V, (h + 1) * GLA_HEAD_V)
            qm = _parts(jnp.where(lane_head == h, qg, 0.0), precise)
            sc = jnp.where(causal, _mm(qm, kg, _dot_nt), 0.0)
            o_ref[rs, vs] = _mm(_parts(sc, precise), tuple(p[:, vs] for p in vp))

    ones_cv = (jnp.ones((CHUNK, GLA_HEAD_V), BF16),)
    bd_row = lax.broadcasted_iota(jnp.int32, (GLA_QK, GLA_DV), 0) // GLA_HEAD_K
    bd_col = lax.broadcasted_iota(jnp.int32, (GLA_QK, GLA_DV), 1) // GLA_HEAD_V
    head_diag = bd_row == bd_col
    if not sample:
        s = jnp.where(seq_start, 0.0, s_ref[...])
    for c in range(CHUNKS_PER_BLOCK):
        cs = (slice(c * CHUNK, (c + 1) * CHUNK), slice(None))
        if sample:
            s = statein_ref[c]
        s_rows = s.reshape(GLA_QK, GLA_HEAD_V)
        s_bd = jnp.where(head_diag, jnp.concatenate([s_rows] * GLA_HEADS, axis=1), 0.0)
        o_ref[cs] += _mm(_load(qg_refs, cs), _parts(s_bd, precise))
        ds = _mm(_load(kd_refs, cs), _load(v_refs, cs), _dot_tn)
        dec = jnp.exp(_mm(_load(la_refs, cs), ones_cv, _dot_tn))
        ds_diag = jnp.stack([ds[h * GLA_HEAD_K:(h + 1) * GLA_HEAD_K, h * GLA_HEAD_V:(h + 1) * GLA_HEAD_V]
                             for h in range(GLA_HEADS)])
        s = dec.reshape(GLA_HEADS, GLA_HEAD_K, GLA_HEAD_V) * s + ds_diag
        if sample:
            stateout_ref[c] = s
    if not sample:
        s_ref[...] = s
        stateout_ref[0] = s

    gn = gnorm_ref[...]
    for h in range(GLA_HEADS):
        vs = slice(h * GLA_HEAD_V, (h + 1) * GLA_HEAD_V)
        oh = o_ref[:, vs]
        ms = jnp.mean(oh * oh, axis=-1, keepdims=True)
        g = proj_ref[:, COL_G + h * GLA_HEAD_V:COL_G + (h + 1) * GLA_HEAD_V]
        y_ref[:, CONV_DIM + h * GLA_HEAD_V:CONV_DIM + (h + 1) * GLA_HEAD_V] = (
            oh * lax.rsqrt(ms + LN_EPS) * gn * _silu(g))

    m = _mm(_parts(y_ref[...], precise), _load(w_out_refs))
    x1 = _layer_norm(ALPHA * x_ref[...] + m, lng_ref[...], lnb_ref[...])
    x1_ref[...] = x1

    if with_router:
        x1b_ref[...] = x1.astype(BF16)
        lg = _mm(_split_bf16(x1), (rhi_ref[...], rlo_ref[...]))
        lane = lax.broadcasted_iota(jnp.int32, lg.shape, 1)
        neg = jnp.float32(-jnp.inf)
        lg = jnp.where(lane < N_EXPERTS, lg, neg)
        m1 = jnp.max(lg, axis=-1, keepdims=True)
        i1 = jnp.min(jnp.where(lg == m1, lane, LANES), axis=-1, keepdims=True)
        lg2 = jnp.where(lane == i1, neg, lg)
        m2 = jnp.max(lg2, axis=-1, keepdims=True)
        i2 = jnp.min(jnp.where(lg2 == m2, lane, LANES), axis=-1, keepdims=True)
        e2 = jnp.exp(m2 - m1)
        g1 = 1.0 / (1.0 + e2)
        g2 = e2 / (1.0 + e2)
        slab = jnp.where(lane == i1, g1, jnp.where(lane == i2, g2, 0.0))
        slab = jnp.where(lane == LANE_I1, i1.astype(F32), slab)
        slab = jnp.where(lane == LANE_I2, i2.astype(F32), slab)
        slab = jnp.where(lane == LANE_G1, g1, slab)
        slab = jnp.where(lane == LANE_G2, g2, slab)
        route_ref[...] = slab


def _mixer_call(x, w, *, n_rows, n_blocks, blocks_per_seq, with_router, sample_state=None, into=None):
    sample = sample_state is not None
    precise = sample
    np_ = 2 if precise else 1
    const2 = lambda i: (0, 0)
    const3 = lambda i: (0, 0, 0)
    const4 = lambda i: (0, 0, 0, 0)

    x_last = x.shape[0] // TB - 1
    in_specs = [pl.BlockSpec((TB, D_MODEL), (lambda i: (x_last, 0)) if sample else (lambda i: (i, 0)))]
    args = [x]
    if sample:
        conv_in, state_in = sample_state
        in_specs += [pl.BlockSpec(conv_in.shape, const3), pl.BlockSpec(state_in.shape, const4)]
        args += [conv_in, state_in]
    in_specs += [pl.BlockSpec((D_MODEL, IN_COLS_PAD), const2)] * np_
    args += list(w["w_in"][:np_])
    in_specs += [pl.BlockSpec((LANES, GLA_QK), const2)] * np_
    args += list(w["wa2"][:np_])
    in_specs += [pl.BlockSpec((1, GLA_QK), const2), pl.BlockSpec((CONV_W, CONV_DIM), const2),
                 pl.BlockSpec((1, GLA_HEAD_V), const2)]
    args += [w["b_alpha"], w["conv_w"], w["gla_norm_g"]]
    in_specs += [pl.BlockSpec((D_MODEL, D_MODEL), const2)] * np_
    args += list(w["w_out"][:np_])
    in_specs += [pl.BlockSpec((1, D_MODEL), const2)] * 2
    args += [w["ln_g"], w["ln_b"]]
    if with_router:
        in_specs += [pl.BlockSpec((D_MODEL, LANES), const2)] * 2
        args += list(w["router"])
    aliases = {}
    n_alias = 0
    if sample:
        n_alias = len(into)
        for k, buf in enumerate(into):
            aliases[len(args)] = k
            in_specs.append(pl.BlockSpec(memory_space=pl.ANY))
            args.append(buf)

    out_row = (lambda i: (n_blocks, 0)) if sample else (lambda i: (i, 0))
    out_shape = [jax.ShapeDtypeStruct((n_rows, D_MODEL), F32)]
    out_specs = [pl.BlockSpec((TB, D_MODEL), out_row)]
    if with_router:
        out_shape += [jax.ShapeDtypeStruct((n_rows, D_MODEL), BF16),
                      jax.ShapeDtypeStruct((n_rows, LANES), F32)]
        out_specs += [pl.BlockSpec((TB, D_MODEL), out_row), pl.BlockSpec((TB, LANES), out_row)]
    if sample:
        out_shape += [jax.ShapeDtypeStruct(conv_in.shape, F32), jax.ShapeDtypeStruct(state_in.shape, F32)]
        out_specs += [pl.BlockSpec(conv_in.shape, const3), pl.BlockSpec(state_in.shape, const4)]
    else:
        n_seq = n_blocks // blocks_per_seq
        out_shape += [jax.ShapeDtypeStruct((n_seq, CONV_W - 1, CONV_DIM), F32),
                      jax.ShapeDtypeStruct((n_seq, GLA_HEADS, GLA_HEAD_K, GLA_HEAD_V), F32)]
        out_specs += [pl.BlockSpec((1, CONV_W - 1, CONV_DIM), lambda i: (i // blocks_per_seq, 0, 0)),
                      pl.BlockSpec((1, GLA_HEADS, GLA_HEAD_K, GLA_HEAD_V),
                                   lambda i: (i // blocks_per_seq, 0, 0, 0))]
    scratch = [
        pltpu.VMEM((TB, IN_COLS_PAD), F32),
        pltpu.VMEM((TB + 8, CONV_DIM), F32),
        pltpu.VMEM((8, CONV_DIM), F32),
        pltpu.VMEM((CHUNKS_PER_BLOCK, 16, CONV_DIM), F32),
        pltpu.VMEM((TB, D_MODEL), F32),
        pltpu.VMEM((TB, GLA_DV), F32),
    ]
    scratch += [pltpu.VMEM((TB, GLA_QK), BF16)] * np_
    scratch += [pltpu.VMEM((TB, GLA_QK), BF16)] * np_
    scratch += [pltpu.VMEM((TB, GLA_DV), BF16)] * np_
    scratch += [pltpu.VMEM((TB, GLA_QK), BF16)] * 2
    scratch += [pltpu.VMEM((GLA_HEADS, GLA_HEAD_K, GLA_HEAD_V), F32)]
    kern = functools.partial(_mixer_kernel, sample=sample, precise=precise, blocks_per_seq=blocks_per_seq,
                             with_router=with_router, n_alias=n_alias)
    name = ("mixer_sample" if sample else "mixer_prompt") + ("_router" if with_router else "")
    return pl.pallas_call(
        kern,
        grid=(1 if sample else n_blocks,),
        in_specs=in_specs,
        out_specs=out_specs,
        out_shape=out_shape,
        scratch_shapes=scratch,
        input_output_aliases=aliases,
        compiler_params=pltpu.CompilerParams(dimension_semantics=("arbitrary",),
                                             vmem_limit_bytes=VMEM_LIMIT_BYTES),
        name=name,
    )(*args)


def _swiglu_step(xp, wg, wu, wd, acc_ref, precise):
    f = pl.program_id(1)
    g = _mm(xp, _parts(wg, precise))
    u = _mm(xp, _parts(wu, precise))
    part = _mm(_parts(_silu(g) * u, precise), _parts(wd, precise))

    @pl.when(f == 0)
    def _():
        acc_ref[...] = part

    @pl.when(f > 0)
    def _():
        acc_ref[...] += part


def _ffn_dense_kernel(*refs, precise, n_alias):
    x_ref, wg_ref, wu_ref, wd_ref, lng_ref, lnb_ref = refs[:6]
    out_ref, acc_ref = refs[6 + n_alias:]
    x = x_ref[...]
    _swiglu_step(_parts(x, precise), wg_ref[0], wu_ref[0], wd_ref[0], acc_ref, precise)

    @pl.when(pl.program_id(1) == pl.num_programs(1) - 1)
    def _():
        out_ref[...] = _layer_norm(ALPHA * x + acc_ref[...], lng_ref[...], lnb_ref[...])


def _ffn_dense_call(x, wg, wu, wd, e, ln_g, ln_b, *, tm, first_block, n_blocks, precise, into=None):
    n_rows = x.shape[0]
    nf = D_FF // TF
    row_spec = pl.BlockSpec((tm, D_MODEL), lambda t, f: (first_block + t, 0))
    in_specs = [row_spec,
                pl.BlockSpec((1, D_MODEL, TF), lambda t, f: (e, 0, f)),
                pl.BlockSpec((1, D_MODEL, TF), lambda t, f: (e, 0, f)),
                pl.BlockSpec((1, TF, D_MODEL), lambda t, f: (e, f, 0)),
                pl.BlockSpec((1, D_MODEL), lambda t, f: (0, 0)),
                pl.BlockSpec((1, D_MODEL), lambda t, f: (0, 0))]
    args = [x, wg, wu, wd, ln_g, ln_b]
    aliases = {}
    n_alias = 0
    if into is not None:
        n_alias = 1
        aliases[len(args)] = 0
        in_specs.append(pl.BlockSpec(memory_space=pl.ANY))
        args.append(into)
    return pl.pallas_call(
        functools.partial(_ffn_dense_kernel, precise=precise, n_alias=n_alias),
        grid=(n_blocks, nf),
        in_specs=in_specs,
        out_specs=row_spec,
        out_shape=jax.ShapeDtypeStruct((n_rows, D_MODEL), F32),
        scratch_shapes=[pltpu.VMEM((tm, D_MODEL), F32)],
        input_output_aliases=aliases,
        compiler_params=pltpu.CompilerParams(dimension_semantics=("arbitrary", "arbitrary"),
                                             vmem_limit_bytes=VMEM_LIMIT_BYTES),
        name="ffn_dense_precise" if precise else "ffn_dense",
    )(*args)


def _ffn_moe_kernel(te_ref, nt_ref, x_ref, wg_ref, wu_ref, wd_ref, out_ref, acc_ref):
    t = pl.program_id(0)
    used = t < nt_ref[0]

    @pl.when(used)
    def _():
        _swiglu_step((x_ref[...],), wg_ref[0, 0], wu_ref[0, 0], wd_ref[0, 0], acc_ref, False)

    last = pl.program_id(1) == pl.num_programs(1) - 1

    @pl.when(jnp.logical_and(last, used))
    def _():
        out_ref[...] = acc_ref[...]

    @pl.when(jnp.logical_and(last, jnp.logical_not(used)))
    def _():
        out_ref[...] = jnp.zeros_like(out_ref)


def _ffn_moe_call(tile_expert, n_tiles_used, xs, wg, wu, wd, e):
    n_rows = xs.shape[0]
    nf = D_FF // TF
    grid_spec = pltpu.PrefetchScalarGridSpec(
        num_scalar_prefetch=2,
        grid=(n_rows // TM_MOE, nf),
        in_specs=[
            pl.BlockSpec((TM_MOE, D_MODEL), lambda t, f, te, nt: (t, 0)),
            pl.BlockSpec((1, 1, D_MODEL, TF), lambda t, f, te, nt: (e, te[t], 0, f)),
            pl.BlockSpec((1, 1, D_MODEL, TF), lambda t, f, te, nt: (e, te[t], 0, f)),
            pl.BlockSpec((1, 1, TF, D_MODEL), lambda t, f, te, nt: (e, te[t], f, 0)),
        ],
        out_specs=pl.BlockSpec((TM_MOE, D_MODEL), lambda t, f, te, nt: (t, 0)),
        scratch_shapes=[pltpu.VMEM((TM_MOE, D_MODEL), F32)],
    )
    return pl.pallas_call(
        _ffn_moe_kernel,
        grid_spec=grid_spec,
        out_shape=jax.ShapeDtypeStruct((n_rows, D_MODEL), F32),
        compiler_params=pltpu.CompilerParams(dimension_semantics=("arbitrary", "arbitrary"),
                                             vmem_limit_bytes=VMEM_LIMIT_BYTES),
        name="ffn_moe",
    )(tile_expert, n_tiles_used, xs, wg, wu, wd)


def _combine_kernel(x_ref, ya_ref, yb_ref, route_ref, lng_ref, lnb_ref, out_ref):
    slab = route_ref[...]
    lane = lax.broadcasted_iota(jnp.int32, slab.shape, 1)
    g1 = jnp.sum(jnp.where(lane == LANE_G1, slab, 0.0), axis=-1, keepdims=True)
    g2 = jnp.sum(jnp.where(lane == LANE_G2, slab, 0.0), axis=-1, keepdims=True)
    r = ALPHA * x_ref[...] + (g1 * ya_ref[...] + g2 * yb_ref[...])
    out_ref[...] = _layer_norm(r, lng_ref[...], lnb_ref[...])


def _combine_call(x, ya, yb, route, ln_g, ln_b, *, first_block, n_blocks):
    in_spec = pl.BlockSpec((TB, D_MODEL), lambda t: (first_block + t, 0))
    route_spec = pl.BlockSpec((TB, LANES), lambda t: (first_block + t, 0))
    vec_spec = pl.BlockSpec((1, D_MODEL), lambda t: (0, 0))
    return pl.pallas_call(
        _combine_kernel,
        grid=(n_blocks,),
        in_specs=[in_spec, in_spec, in_spec, route_spec, vec_spec, vec_spec],
        out_specs=pl.BlockSpec((TB, D_MODEL), lambda t: (t, 0)),
        out_shape=jax.ShapeDtypeStruct((n_blocks * TB, D_MODEL), F32),
        compiler_params=pltpu.CompilerParams(dimension_semantics=("arbitrary",),
                                             vmem_limit_bytes=VMEM_LIMIT_BYTES),
        name="moe_combine",
    )(x, ya, yb, route, ln_g, ln_b)


def _take_rows(a, idx):
    return a.at[idx].get(mode="promise_in_bounds")


def _moe_dispatch(route, n_rows):
    i1 = route[:, LANE_I1].astype(jnp.int32)
    i2 = route[:, LANE_I2].astype(jnp.int32)
    experts = jnp.arange(N_EXPERTS, dtype=jnp.int32)
    hit = ((i1[:, None] == experts) | (i2[:, None] == experts)).astype(jnp.int32)
    before = jnp.cumsum(hit, axis=0) - hit
    counts = jnp.sum(hit, axis=0)
    padded = ((counts + TM_MOE - 1) // TM_MOE) * TM_MOE
    ends = jnp.cumsum(padded)
    starts = ends - padded
    rank1 = jnp.sum(jnp.where(i1[:, None] == experts, before, 0), axis=1)
    rank2 = jnp.sum(jnp.where(i2[:, None] == experts, before, 0), axis=1)
    p1 = _take_rows(starts, i1) + rank1
    p2 = _take_rows(starts, i2) + rank2
    n_tiles = (2 * n_rows + N_EXPERTS * (TM_MOE - 1) + TM_MOE - 1) // TM_MOE
    n_sorted = n_tiles * TM_MOE
    tok = jnp.arange(n_rows, dtype=jnp.int32)
    row_src = jnp.zeros((n_sorted,), jnp.int32).at[p1].set(tok, mode="promise_in_bounds")
    row_src = row_src.at[p2].set(tok, mode="promise_in_bounds")
    tile_start = jnp.arange(n_tiles, dtype=jnp.int32) * TM_MOE
    tile_expert = jnp.minimum(jnp.sum(tile_start[:, None] >= ends[None, :], axis=1), N_EXPERTS - 1)
    n_tiles_used = (ends[-1] // TM_MOE).reshape(1)
    return row_src, p1, p2, tile_expert.astype(jnp.int32), n_tiles_used.astype(jnp.int32)


def _split_kernel(a_ref, hi_ref, lo_ref):
    hi, lo = _split_bf16(a_ref[...])
    hi_ref[...] = hi
    lo_ref[...] = lo


def _split_call(a):
    rows, cols = a.shape
    tr = min(rows, 256)
    spec = pl.BlockSpec((tr, cols), lambda r: (r, 0))
    return pl.pallas_call(
        _split_kernel,
        grid=(rows // tr,),
        in_specs=[spec],
        out_specs=[spec, spec],
        out_shape=[jax.ShapeDtypeStruct(a.shape, BF16)] * 2,
        name="split_bf16",
    )(a)


def _pad_w_in(w_in):
    main = w_in[:, :COL_A]
    alr = jnp.pad(w_in[:, COL_A:], ((0, 0), (0, LANES - GATE_RANK)))
    return jnp.concatenate([main, alr], axis=1)


def kernel(x_prompt, x_sample, cache_conv, state_gla, ln_mix_g, ln_mix_b, w_in, w_alpha2, b_alpha,
           conv_w, gla_norm_g, w_out, ln_ffn_g, ln_ffn_b, ffn_w_gate, ffn_w_up, ffn_w_down,
           moe_router, moe_w_gate, moe_w_up, moe_w_down):
    n_prompt_seq, seq_len, _ = x_prompt.shape
    n_sample, dec_len, _ = x_sample.shape
    assert dec_len == CHUNK and n_sample * dec_len == TB and seq_len % TB == 0
    n_prompt_rows = n_prompt_seq * seq_len
    n_rows = n_prompt_rows + TB
    blocks_per_seq = seq_len // TB
    n_prompt_blocks = n_prompt_rows // TB
    assert n_prompt_rows % TM_DENSE == 0

    xp = x_prompt.reshape(n_prompt_rows, D_MODEL)
    xs = x_sample.reshape(TB, D_MODEL)
    gla_p, conv_p, gla_s, conv_s = [], [], [], []
    for l in range(DEPTH):
        moe_layer = l % 2 == 1
        e = l // 2
        w = {
            "w_in": _split_call(_pad_w_in(w_in[l])),
            "wa2": _split_call(jnp.pad(w_alpha2[l], ((0, LANES - GATE_RANK), (0, 0)))),
            "b_alpha": b_alpha[l][None, :],
            "conv_w": conv_w[l],
            "gla_norm_g": gla_norm_g[l][None, :],
            "w_out": _split_call(w_out[l]),
            "ln_g": ln_mix_g[l][None, :],
            "ln_b": ln_mix_b[l][None, :],
        }
        if moe_layer:
            w["router"] = _split_call(jnp.pad(moe_router[e], ((0, 0), (0, LANES - N_EXPERTS))))
        common = dict(n_rows=n_rows, n_blocks=n_prompt_blocks, blocks_per_seq=blocks_per_seq,
                      with_router=moe_layer)
        outs = _mixer_call(xp, w, **common)
        n_stream = 3 if moe_layer else 1
        cp, sp = outs[n_stream:]
        outs_s = _mixer_call(xs, w, sample_state=(cache_conv[l], state_gla[l]), into=outs[:n_stream], **common)
        cs, ss = outs_s[n_stream:]
        conv_p.append(cp)
        gla_p.append(sp)
        conv_s.append(cs)
        gla_s.append(ss)

        ln_g = ln_ffn_g[l][None, :]
        ln_b = ln_ffn_b[l][None, :]
        if moe_layer:
            x1, x1b, route = outs_s[:3]
            row_src, p1, p2, tile_expert, n_tiles_used = _moe_dispatch(route, n_rows)
            ys = _ffn_moe_call(tile_expert, n_tiles_used, _take_rows(x1b, row_src),
                               moe_w_gate, moe_w_up, moe_w_down, e)
            comb = functools.partial(_combine_call, x1, _take_rows(ys, p1), _take_rows(ys, p2), route, ln_g, ln_b)
            if l == DEPTH - 1:
                xp = comb(first_block=0, n_blocks=n_prompt_blocks)
                xs = comb(first_block=n_prompt_blocks, n_blocks=1)
            else:
                xp = xs = comb(first_block=0, n_blocks=n_prompt_blocks + 1)
        else:
            x1 = outs_s[0]
            dense = functools.partial(_ffn_dense_call, x1, ffn_w_gate, ffn_w_up, ffn_w_down, e, ln_g, ln_b)
            x2 = dense(tm=TM_DENSE, first_block=0, n_blocks=n_prompt_rows // TM_DENSE, precise=False)
            xp = xs = dense(tm=TB, first_block=n_prompt_blocks, n_blocks=1, precise=(l == 0), into=x2)

    y_prompt = xp[:n_prompt_rows].reshape(n_prompt_seq, seq_len, D_MODEL)
    y_sample = xs[xs.shape[0] - TB:].reshape(n_sample, dec_len, D_MODEL)
    return (y_prompt, y_sample, jnp.stack(gla_p), jnp.stack(conv_p), jnp.stack(gla_s), jnp.stack(conv_s))
```

```python
import functools

import jax
import jax.numpy as jnp
from jax import lax
from jax.experimental import pallas as pl
from jax.experimental.pallas import tpu as pltpu

F32 = jnp.float32
BF16 = jnp.bfloat16

D_MODEL = 1024
DEPTH = 4
CHUNK = 64
CONV_DIM = 512
CONV_W = 3
GLA_HEADS = 4
GLA_HEAD_K = 64
GLA_HEAD_V = 128
GLA_QK = GLA_HEADS * GLA_HEAD_K
GLA_DV = GLA_HEADS * GLA_HEAD_V
GATE_RANK = 16
GATE_TEMP = 16.0
D_FF = 3584
N_EXPERTS = 8
ALPHA = (2 * DEPTH) ** 0.25
LN_EPS = 1e-5

LANES = 128
SUBLANES = 8
VMEM_LIMIT_BYTES = 56 * 1024 * 1024

COL_CB, COL_CC, COL_CH = 0, CONV_DIM, 2 * CONV_DIM
COL_Q = 3 * CONV_DIM
COL_K = COL_Q + GLA_QK
COL_V = COL_K + GLA_QK
COL_G = COL_V + GLA_DV
COL_A = COL_G + GLA_DV
IN_COLS_PAD = COL_A + LANES

TB = 512
SUB = 256
CHUNKS_PER_BLOCK = TB // CHUNK
TF = 512
TM_MOE = 1024
TM_DENSE = 1024
LANE_I1, LANE_I2, LANE_G1, LANE_G2 = 8, 9, 10, 11


def _dot(a, b):
    return jnp.dot(a, b, preferred_element_type=F32)


def _dot_tn(a, b):
    return lax.dot_general(a, b, (((0,), (0,)), ((), ())), preferred_element_type=F32)


def _dot_nt(a, b):
    return lax.dot_general(a, b, (((1,), (1,)), ((), ())), preferred_element_type=F32)


def _split_bf16(a):
    hi = a.astype(BF16)
    lo = (a - hi.astype(F32)).astype(BF16)
    return hi, lo


def _parts(a, precise):
    return _split_bf16(a) if precise else (a.astype(BF16),)


def _mm(a, b, dot=_dot):
    r = dot(a[0], b[0])
    if len(a) > 1:
        r = r + dot(a[1], b[0])
    if len(b) > 1:
        r = r + dot(a[0], b[1])
    return r


def _load(refs, idx=None):
    return tuple(r[...] if idx is None else r[idx] for r in refs)


def _store(refs, idx, parts):
    for r, p in zip(refs, parts):
        r[idx] = p


def _layer_norm(r, g, b):
    mu = jnp.mean(r, axis=-1, keepdims=True)
    d = r - mu
    var = jnp.mean(d * d, axis=-1, keepdims=True)
    return d * lax.rsqrt(var + LN_EPS) * g + b


def _silu(g):
    return g / (1.0 + jnp.exp(-g))


def _mixer_kernel(*refs, sample, precise, blocks_per_seq, with_router, n_alias):
    np_ = 2 if precise else 1
    it = iter(refs)

    def take(n):
        return [next(it) for _ in range(n)]

    x_ref = next(it)
    if sample:
        convin_ref = next(it)
        statein_ref = next(it)
    w_in_refs = take(np_)
    wa2_refs = take(np_)
    balpha_ref = next(it)
    convw_ref = next(it)
    gnorm_ref = next(it)
    w_out_refs = take(np_)
    lng_ref = next(it)
    lnb_ref = next(it)
    if with_router:
        rhi_ref = next(it)
        rlo_ref = next(it)
    take(n_alias)
    x1_ref = next(it)
    if with_router:
        route_ref = next(it)
    convout_ref = next(it)
    stateout_ref = next(it)
    proj_ref = next(it)
    ue_ref = next(it)
    prev_ref = next(it)
    fx_ref = next(it)
    y_ref = next(it)
    o_ref = next(it)
    qg_refs = take(np_)
    kd_refs = take(np_)
    v_refs = take(np_)
    la_refs = take(2)
    s_ref = next(it)

    i = pl.program_id(0)
    seq_start = (i % blocks_per_seq) == 0

    proj_ref[...] = _mm(_parts(x_ref[...], precise), _load(w_in_refs))

    if sample:
        prev_ref[6:8, :] = convin_ref[0]
    else:
        @pl.when(seq_start)
        def _():
            prev_ref[...] = jnp.zeros_like(prev_ref)

    w0 = convw_ref[0:1, :]
    w1 = convw_ref[1:2, :]
    w2 = convw_ref[2:3, :]
    u = proj_ref[:, COL_CC:COL_CC + CONV_DIM] * proj_ref[:, COL_CH:COL_CH + CONV_DIM]
    ue_ref[0:8, :] = prev_ref[...]
    ue_ref[8:8 + TB, :] = u
    conv = ue_ref[6:6 + TB, :] * w0 + ue_ref[7:7 + TB, :] * w1 + u * w2
    y_ref[:, 0:CONV_DIM] = proj_ref[:, COL_CB:COL_CB + CONV_DIM] * conv
    if sample:
        for c in range(CHUNKS_PER_BLOCK):
            r0 = c * CHUNK
            if c > 0:
                fx_ref[c, 6:8, :] = convin_ref[c]
                fx_ref[c, 8:16, :] = ue_ref[8 + r0:16 + r0, :]
                convc = fx_ref[c, 6:14, :] * w0 + fx_ref[c, 7:15, :] * w1 + fx_ref[c, 8:16, :] * w2
                y_ref[r0:r0 + 8, 0:CONV_DIM] = proj_ref[r0:r0 + 8, COL_CB:COL_CB + CONV_DIM] * convc
            convout_ref[c] = ue_ref[8 + r0 + CHUNK - 2:8 + r0 + CHUNK, :]
    else:
        prev_ref[...] = ue_ref[TB:TB + 8, :]
        convout_ref[0] = ue_ref[TB + 6:TB + 8, :]

    z = _mm(_parts(proj_ref[:, COL_A:COL_A + LANES], precise), _load(wa2_refs)) + balpha_ref[...]
    la = (jnp.minimum(z, 0.0) - jnp.log(1.0 + jnp.exp(-jnp.abs(z)))) * (1.0 / GATE_TEMP)
    _store(la_refs, slice(None), _split_bf16(la))

    row = lax.broadcasted_iota(jnp.int32, (SUB, SUB), 0)
    col = lax.broadcasted_iota(jnp.int32, (SUB, SUB), 1)
    same_chunk = (row // CHUNK) == (col // CHUNK)
    causal = jnp.logical_and(same_chunk, row >= col)
    tril_bd = (causal.astype(BF16),)
    ones_bd = (same_chunk.astype(BF16),)
    lane_head = lax.broadcasted_iota(jnp.int32, (SUB, GLA_QK), 1) // GLA_HEAD_K

    for sb in range(TB // SUB):
        rs = slice(sb * SUB, (sb + 1) * SUB)
        lap = _load(la_refs, (rs, slice(None)))
        bcum = _mm(tril_bd, lap)
        blast = _mm(ones_bd, lap)
        q = proj_ref[rs, COL_Q:COL_Q + GLA_QK]
        k = proj_ref[rs, COL_K:COL_K + GLA_QK]
        qg = q * jnp.exp(bcum) * (GLA_HEAD_K ** -0.5)
        kg = _parts(k * jnp.exp(-bcum), precise)
        _store(qg_refs, (rs, slice(None)), _parts(qg, precise))
        _store(kd_refs, (rs, slice(None)), _parts(k * jnp.exp(blast - bcum), precise))
        vp = _parts(proj_ref[rs, COL_V:COL_V + GLA_DV], precise)
        _store(v_refs, (rs, slice(None)), vp)
        for h in range(GLA_HEADS):
            vs = slice(h * GLA_HEAD_V, (h + 1) * GLA_HEAD_V)
            qm = _parts(jnp.where(lane_head == h, qg, 0.0), precise)
            sc = jnp.where(causal, _mm(qm, kg, _dot_nt), 0.0)
            o_ref[rs, vs] = _mm(_parts(sc, precise), tuple(p[:, vs] for p in vp))

    ones_cv = (jnp.ones((CHUNK, GLA_HEAD_V), BF16),)
    bd_row = lax.broadcasted_iota(jnp.int32, (GLA_QK, GLA_DV), 0) // GLA_HEAD_K
    bd_col = lax.broadcasted_iota(jnp.int32, (GLA_QK, GLA_DV), 1) // GLA_HEAD_V
    head_diag = bd_row == bd_col
    if not sample:
        s = jnp.where(seq_start, 0.0, s_ref[...])
    for c in range(CHUNKS_PER_BLOCK):
        cs = (slice(c * CHUNK, (c + 1) * CHUNK), slice(None))
        if sample:
            s = statein_ref[c]
        s_rows = s.reshape(GLA_QK, GLA_HEAD_V)
        s_bd = jnp.where(head_diag, jnp.concatenate([s_rows] * GLA_HEADS, axis=1), 0.0)
        o_ref[cs] += _mm(_load(qg_refs, cs), _parts(s_bd, precise))
        ds = _mm(_load(kd_refs, cs), _load(v_refs, cs), _dot_tn)
        dec = jnp.exp(_mm(_load(la_refs, cs), ones_cv, _dot_tn))
        ds_diag = jnp.stack([ds[h * GLA_HEAD_K:(h + 1) * GLA_HEAD_K, h * GLA_HEAD_V:(h + 1) * GLA_HEAD_V]
                             for h in range(GLA_HEADS)])
        s = dec.reshape(GLA_HEADS, GLA_HEAD_K, GLA_HEAD_V) * s + ds_diag
        if sample:
            stateout_ref[c] = s
    if not sample:
        s_ref[...] = s
        stateout_ref[0] = s

    gn = gnorm_ref[...]
    for h in range(GLA_HEADS):
        vs = slice(h * GLA_HEAD_V, (h + 1) * GLA_HEAD_V)
        oh = o_ref[:, vs]
        ms = jnp.mean(oh * oh, axis=-1, keepdims=True)
        g = proj_ref[:, COL_G + h * GLA_HEAD_V:COL_G + (h + 1) * GLA_HEAD_V]
        y_ref[:, CONV_DIM + h * GLA_HEAD_V:CONV_DIM + (h + 1) * GLA_HEAD_V] = (
            oh * lax.rsqrt(ms + LN_EPS) * gn * _silu(g))

    m = _mm(_parts(y_ref[...], precise), _load(w_out_refs))
    x1 = _layer_norm(ALPHA * x_ref[...] + m, lng_ref[...], lnb_ref[...])
    x1_ref[...] = x1

    if with_router:
        lg = _mm(_split_bf16(x1), (rhi_ref[...], rlo_ref[...]))
        lane = lax.broadcasted_iota(jnp.int32, lg.shape, 1)
        neg = jnp.float32(-jnp.inf)
        lg = jnp.where(lane < N_EXPERTS, lg, neg)
        m1 = jnp.max(lg, axis=-1, keepdims=True)
        i1 = jnp.min(jnp.where(lg == m1, lane, LANES), axis=-1, keepdims=True)
        lg2 = jnp.where(lane == i1, neg, lg)
        m2 = jnp.max(lg2, axis=-1, keepdims=True)
        i2 = jnp.min(jnp.where(lg2 == m2, lane, LANES), axis=-1, keepdims=True)
        e2 = jnp.exp(m2 - m1)
        g1 = 1.0 / (1.0 + e2)
        g2 = e2 / (1.0 + e2)
        slab = jnp.where(lane == i1, g1, jnp.where(lane == i2, g2, 0.0))
        slab = jnp.where(lane == LANE_I1, i1.astype(F32), slab)
        slab = jnp.where(lane == LANE_I2, i2.astype(F32), slab)
        slab = jnp.where(lane == LANE_G1, g1, slab)
        slab = jnp.where(lane == LANE_G2, g2, slab)
        route_ref[...] = slab


def _mixer_call(x, w, *, n_rows, n_blocks, blocks_per_seq, with_router, sample_state=None, into=None):
    sample = sample_state is not None
    precise = sample
    np_ = 2 if precise else 1
    const2 = lambda i: (0, 0)
    const3 = lambda i: (0, 0, 0)
    const4 = lambda i: (0, 0, 0, 0)

    x_last = x.shape[0] // TB - 1
    in_specs = [pl.BlockSpec((TB, D_MODEL), (lambda i: (x_last, 0)) if sample else (lambda i: (i, 0)))]
    args = [x]
    if sample:
        conv_in, state_in = sample_state
        in_specs += [pl.BlockSpec(conv_in.shape, const3), pl.BlockSpec(state_in.shape, const4)]
        args += [conv_in, state_in]
    in_specs += [pl.BlockSpec((D_MODEL, IN_COLS_PAD), const2)] * np_
    args += list(w["w_in"][:np_])
    in_specs += [pl.BlockSpec((LANES, GLA_QK), const2)] * np_
    args += list(w["wa2"][:np_])
    in_specs += [pl.BlockSpec((1, GLA_QK), const2), pl.BlockSpec((CONV_W, CONV_DIM), const2),
                 pl.BlockSpec((1, GLA_HEAD_V), const2)]
    args += [w["b_alpha"], w["conv_w"], w["gla_norm_g"]]
    in_specs += [pl.BlockSpec((D_MODEL, D_MODEL), const2)] * np_
    args += list(w["w_out"][:np_])
    in_specs += [pl.BlockSpec((1, D_MODEL), const2)] * 2
    args += [w["ln_g"], w["ln_b"]]
    if with_router:
        in_specs += [pl.BlockSpec((D_MODEL, LANES), const2)] * 2
        args += list(w["router"])
    aliases = {}
    n_alias = 0
    if sample:
        n_alias = len(into)
        for k, buf in enumerate(into):
            aliases[len(args)] = k
            in_specs.append(pl.BlockSpec(memory_space=pl.ANY))
            args.append(buf)

    out_row = (lambda i: (n_blocks, 0)) if sample else (lambda i: (i, 0))
    out_shape = [jax.ShapeDtypeStruct((n_rows, D_MODEL), F32)]
    out_specs = [pl.BlockSpec((TB, D_MODEL), out_row)]
    if with_router:
        out_shape += [jax.ShapeDtypeStruct((n_rows, LANES), F32)]
        out_specs += [pl.BlockSpec((TB, LANES), out_row)]
    if sample:
        out_shape += [jax.ShapeDtypeStruct(conv_in.shape, F32), jax.ShapeDtypeStruct(state_in.shape, F32)]
        out_specs += [pl.BlockSpec(conv_in.shape, const3), pl.BlockSpec(state_in.shape, const4)]
    else:
        n_seq = n_blocks // blocks_per_seq
        out_shape += [jax.ShapeDtypeStruct((n_seq, CONV_W - 1, CONV_DIM), F32),
                      jax.ShapeDtypeStruct((n_seq, GLA_HEADS, GLA_HEAD_K, GLA_HEAD_V), F32)]
        out_specs += [pl.BlockSpec((1, CONV_W - 1, CONV_DIM), lambda i: (i // blocks_per_seq, 0, 0)),
                      pl.BlockSpec((1, GLA_HEADS, GLA_HEAD_K, GLA_HEAD_V),
                                   lambda i: (i // blocks_per_seq, 0, 0, 0))]
    scratch = [
        pltpu.VMEM((TB, IN_COLS_PAD), F32),
        pltpu.VMEM((TB + 8, CONV_DIM), F32),
        pltpu.VMEM((8, CONV_DIM), F32),
        pltpu.VMEM((CHUNKS_PER_BLOCK, 16, CONV_DIM), F32),
        pltpu.VMEM((TB, D_MODEL), F32),
        pltpu.VMEM((TB, GLA_DV), F32),
    ]
    scratch += [pltpu.VMEM((TB, GLA_QK), BF16)] * np_
    scratch += [pltpu.VMEM((TB, GLA_QK), BF16)] * np_
    scratch += [pltpu.VMEM((TB, GLA_DV), BF16)] * np_
    scratch += [pltpu.VMEM((TB, GLA_QK), BF16)] * 2
    scratch += [pltpu.VMEM((GLA_HEADS, GLA_HEAD_K, GLA_HEAD_V), F32)]
    kern = functools.partial(_mixer_kernel, sample=sample, precise=precise, blocks_per_seq=blocks_per_seq,
                             with_router=with_router, n_alias=n_alias)
    name = ("mixer_sample" if sample else "mixer_prompt") + ("_router" if with_router else "")
    return pl.pallas_call(
        kern,
        grid=(1 if sample else n_blocks,),
        in_specs=in_specs,
        out_specs=out_specs,
        out_shape=out_shape,
        scratch_shapes=scratch,
        input_output_aliases=aliases,
        compiler_params=pltpu.CompilerParams(dimension_semantics=("arbitrary",),
                                             vmem_limit_bytes=VMEM_LIMIT_BYTES),
        name=name,
    )(*args)


def _swiglu_step(xp, wg, wu, wd, acc_ref, precise):
    f = pl.program_id(1)
    g = _mm(xp, _parts(wg, precise))
    u = _mm(xp, _parts(wu, precise))
    part = _mm(_parts(_silu(g) * u, precise), _parts(wd, precise))

    @pl.when(f == 0)
    def _():
        acc_ref[...] = part

    @pl.when(f > 0)
    def _():
        acc_ref[...] += part


def _ffn_dense_kernel(*refs, precise, n_alias):
    x_ref, wg_ref, wu_ref, wd_ref, lng_ref, lnb_ref = refs[:6]
    out_ref, acc_ref = refs[6 + n_alias:]
    x = x_ref[...]
    _swiglu_step(_parts(x, precise), wg_ref[0], wu_ref[0], wd_ref[0], acc_ref, precise)

    @pl.when(pl.program_id(1) == pl.num_programs(1) - 1)
    def _():
        out_ref[...] = _layer_norm(ALPHA * x + acc_ref[...], lng_ref[...], lnb_ref[...])


def _ffn_dense_call(x, wg, wu, wd, e, ln_g, ln_b, *, tm, first_block, n_blocks, precise, into=None):
    n_rows = x.shape[0]
    nf = D_FF // TF
    row_spec = pl.BlockSpec((tm, D_MODEL), lambda t, f: (first_block + t, 0))
    in_specs = [row_spec,
                pl.BlockSpec((1, D_MODEL, TF), lambda t, f: (e, 0, f)),
                pl.BlockSpec((1, D_MODEL, TF), lambda t, f: (e, 0, f)),
                pl.BlockSpec((1, TF, D_MODEL), lambda t, f: (e, f, 0)),
                pl.BlockSpec((1, D_MODEL), lambda t, f: (0, 0)),
                pl.BlockSpec((1, D_MODEL), lambda t, f: (0, 0))]
    args = [x, wg, wu, wd, ln_g, ln_b]
    aliases = {}
    n_alias = 0
    if into is not None:
        n_alias = 1
        aliases[len(args)] = 0
        in_specs.append(pl.BlockSpec(memory_space=pl.ANY))
        args.append(into)
    return pl.pallas_call(
        functools.partial(_ffn_dense_kernel, precise=precise, n_alias=n_alias),
        grid=(n_blocks, nf),
        in_specs=in_specs,
        out_specs=row_spec,
        out_shape=jax.ShapeDtypeStruct((n_rows, D_MODEL), F32),
        scratch_shapes=[pltpu.VMEM((tm, D_MODEL), F32)],
        input_output_aliases=aliases,
        compiler_params=pltpu.CompilerParams(dimension_semantics=("arbitrary", "arbitrary"),
                                             vmem_limit_bytes=VMEM_LIMIT_BYTES),
        name="ffn_dense_precise" if precise else "ffn_dense",
    )(*args)


def _ffn_moe_kernel(te_ref, nv_ref, x_ref, wg_ref, wu_ref, wd_ref, out_ref, acc_ref, xb_ref):
    t = pl.program_id(0)
    f = pl.program_id(1)
    n_valid = nv_ref[t]
    used = n_valid > 0

    @pl.when(jnp.logical_and(used, f == 0))
    def _():
        row = lax.broadcasted_iota(jnp.int32, (TM_MOE, 1), 0)
        xb_ref[...] = jnp.where(row < n_valid, x_ref[...], 0.0).astype(BF16)

    @pl.when(used)
    def _():
        _swiglu_step((xb_ref[...],), wg_ref[0, 0], wu_ref[0, 0], wd_ref[0, 0], acc_ref, False)

    last = f == pl.num_programs(1) - 1

    @pl.when(jnp.logical_and(last, used))
    def _():
        out_ref[...] = acc_ref[...]

    @pl.when(jnp.logical_and(last, jnp.logical_not(used)))
    def _():
        out_ref[...] = jnp.zeros_like(out_ref)


def _ffn_moe_call(tile_expert, tile_valid, xs, wg, wu, wd, e):
    n_rows = xs.shape[0]
    nf = D_FF // TF
    grid_spec = pltpu.PrefetchScalarGridSpec(
        num_scalar_prefetch=2,
        grid=(n_rows // TM_MOE, nf),
        in_specs=[
            pl.BlockSpec((TM_MOE, D_MODEL), lambda t, f, te, nv: (t, 0)),
            pl.BlockSpec((1, 1, D_MODEL, TF), lambda t, f, te, nv: (e, te[t], 0, f)),
            pl.BlockSpec((1, 1, D_MODEL, TF), lambda t, f, te, nv: (e, te[t], 0, f)),
            pl.BlockSpec((1, 1, TF, D_MODEL), lambda t, f, te, nv: (e, te[t], f, 0)),
        ],
        out_specs=pl.BlockSpec((TM_MOE, D_MODEL), lambda t, f, te, nv: (t, 0)),
        scratch_shapes=[pltpu.VMEM((TM_MOE, D_MODEL), F32), pltpu.VMEM((TM_MOE, D_MODEL), BF16)],
    )
    return pl.pallas_call(
        _ffn_moe_kernel,
        grid_spec=grid_spec,
        out_shape=jax.ShapeDtypeStruct((n_rows, D_MODEL), F32),
        compiler_params=pltpu.CompilerParams(dimension_semantics=("arbitrary", "arbitrary"),
                                             vmem_limit_bytes=VMEM_LIMIT_BYTES),
        name="ffn_moe",
    )(tile_expert, tile_valid, xs, wg, wu, wd)


def _dispatch_kernel(pos_ref, x_hbm, out_hbm, pos_smem, sem_pos, sem_rows):
    i = pl.program_id(0)

    def wait_block_of_rows():
        pltpu.make_async_copy(x_hbm.at[pl.ds(0, TB)], out_hbm.at[pl.ds(0, TB)], sem_rows).wait()

    @pl.when(i > 0)
    def _():
        wait_block_of_rows()
        wait_block_of_rows()

    to_smem = pltpu.make_async_copy(pos_ref.at[0], pos_smem, sem_pos)
    to_smem.start()
    to_smem.wait()
    base = i * TB

    def body(r, carry):
        src = x_hbm.at[base + r]
        pltpu.make_async_copy(src, out_hbm.at[pos_smem[0, r]], sem_rows).start()
        pltpu.make_async_copy(src, out_hbm.at[pos_smem[1, r]], sem_rows).start()
        return carry

    lax.fori_loop(0, TB, body, 0, unroll=8)

    @pl.when(i == pl.num_programs(0) - 1)
    def _():
        wait_block_of_rows()
        wait_block_of_rows()


def _dispatch_call(x, pos, n_sorted):
    n_blocks = pos.shape[0]
    return pl.pallas_call(
        _dispatch_kernel,
        grid=(n_blocks,),
        in_specs=[pl.BlockSpec((1, 2, TB), lambda i: (i, 0, 0)),
                  pl.BlockSpec(memory_space=pl.ANY)],
        out_specs=pl.BlockSpec(memory_space=pl.ANY),
        out_shape=jax.ShapeDtypeStruct((n_sorted, D_MODEL), F32),
        scratch_shapes=[pltpu.SMEM((2, TB), jnp.int32), pltpu.SemaphoreType.DMA, pltpu.SemaphoreType.DMA],
        compiler_params=pltpu.CompilerParams(dimension_semantics=("arbitrary",), has_side_effects=True),
        name="moe_dispatch",
    )(pos, x)


def _combine_kernel(x_ref, ya_ref, yb_ref, route_ref, lng_ref, lnb_ref, out_ref):
    slab = route_ref[...]
    lane = lax.broadcasted_iota(jnp.int32, slab.shape, 1)
    g1 = jnp.sum(jnp.where(lane == LANE_G1, slab, 0.0), axis=-1, keepdims=True)
    g2 = jnp.sum(jnp.where(lane == LANE_G2, slab, 0.0), axis=-1, keepdims=True)
    r = ALPHA * x_ref[...] + (g1 * ya_ref[...] + g2 * yb_ref[...])
    out_ref[...] = _layer_norm(r, lng_ref[...], lnb_ref[...])


def _combine_call(x, ya, yb, route, ln_g, ln_b, *, first_block, n_blocks):
    in_spec = pl.BlockSpec((TB, D_MODEL), lambda t: (first_block + t, 0))
    route_spec = pl.BlockSpec((TB, LANES), lambda t: (first_block + t, 0))
    vec_spec = pl.BlockSpec((1, D_MODEL), lambda t: (0, 0))
    return pl.pallas_call(
        _combine_kernel,
        grid=(n_blocks,),
        in_specs=[in_spec, in_spec, in_spec, route_spec, vec_spec, vec_spec],
        out_specs=pl.BlockSpec((TB, D_MODEL), lambda t: (t, 0)),
        out_shape=jax.ShapeDtypeStruct((n_blocks * TB, D_MODEL), F32),
        compiler_params=pltpu.CompilerParams(dimension_semantics=("arbitrary",),
                                             vmem_limit_bytes=VMEM_LIMIT_BYTES),
        name="moe_combine",
    )(x, ya, yb, route, ln_g, ln_b)


def _take_rows(a, idx):
    return a.at[idx].get(mode="promise_in_bounds")


def _moe_dispatch(route, n_rows):
    i1 = route[:, LANE_I1].astype(jnp.int32)
    i2 = route[:, LANE_I2].astype(jnp.int32)
    experts = jnp.arange(N_EXPERTS, dtype=jnp.int32)
    hit = ((i1[:, None] == experts) | (i2[:, None] == experts)).astype(jnp.int32)
    before = jnp.cumsum(hit, axis=0) - hit
    counts = jnp.sum(hit, axis=0)
    padded = ((counts + TM_MOE - 1) // TM_MOE) * TM_MOE
    ends = jnp.cumsum(padded)
    starts = ends - padded
    rank1 = jnp.sum(jnp.where(i1[:, None] == experts, before, 0), axis=1)
    rank2 = jnp.sum(jnp.where(i2[:, None] == experts, before, 0), axis=1)
    p1 = _take_rows(starts, i1) + rank1
    p2 = _take_rows(starts, i2) + rank2
    n_tiles = (2 * n_rows + N_EXPERTS * (TM_MOE - 1) + TM_MOE - 1) // TM_MOE
    tile_start = jnp.arange(n_tiles, dtype=jnp.int32) * TM_MOE
    tile_expert = jnp.minimum(jnp.sum(tile_start[:, None] >= ends[None, :], axis=1), N_EXPERTS - 1)
    tile_expert = tile_expert.astype(jnp.int32)
    group_end = _take_rows(starts + counts, tile_expert)
    tile_valid = jnp.clip(group_end - tile_start, 0, TM_MOE).astype(jnp.int32)
    return p1, p2, tile_expert, tile_valid, n_tiles * TM_MOE


def _split_kernel(a_ref, hi_ref, lo_ref):
    hi, lo = _split_bf16(a_ref[...])
    hi_ref[...] = hi
    lo_ref[...] = lo


def _split_call(a):
    rows, cols = a.shape
    tr = min(rows, 256)
    spec = pl.BlockSpec((tr, cols), lambda r: (r, 0))
    return pl.pallas_call(
        _split_kernel,
        grid=(rows // tr,),
        in_specs=[spec],
        out_specs=[spec, spec],
        out_shape=[jax.ShapeDtypeStruct(a.shape, BF16)] * 2,
        name="split_bf16",
    )(a)


def _pad_w_in(w_in):
    main = w_in[:, :COL_A]
    alr = jnp.pad(w_in[:, COL_A:], ((0, 0), (0, LANES - GATE_RANK)))
    return jnp.concatenate([main, alr], axis=1)


def kernel(x_prompt, x_sample, cache_conv, state_gla, ln_mix_g, ln_mix_b, w_in, w_alpha2, b_alpha,
           conv_w, gla_norm_g, w_out, ln_ffn_g, ln_ffn_b, ffn_w_gate, ffn_w_up, ffn_w_down,
           moe_router, moe_w_gate, moe_w_up, moe_w_down):
    n_prompt_seq, seq_len, _ = x_prompt.shape
    n_sample, dec_len, _ = x_sample.shape
    assert dec_len == CHUNK and n_sample * dec_len == TB and seq_len % TB == 0
    n_prompt_rows = n_prompt_seq * seq_len
    n_rows = n_prompt_rows + TB
    blocks_per_seq = seq_len // TB
    n_prompt_blocks = n_prompt_rows // TB
    assert n_prompt_rows % TM_DENSE == 0

    xp = x_prompt.reshape(n_prompt_rows, D_MODEL)
    xs = x_sample.reshape(TB, D_MODEL)
    gla_p, conv_p, gla_s, conv_s = [], [], [], []
    for l in range(DEPTH):
        moe_layer = l % 2 == 1
        e = l // 2
        w = {
            "w_in": _split_call(_pad_w_in(w_in[l])),
            "wa2": _split_call(jnp.pad(w_alpha2[l], ((0, LANES - GATE_RANK), (0, 0)))),
            "b_alpha": b_alpha[l][None, :],
            "conv_w": conv_w[l],
            "gla_norm_g": gla_norm_g[l][None, :],
            "w_out": _split_call(w_out[l]),
            "ln_g": ln_mix_g[l][None, :],
            "ln_b": ln_mix_b[l][None, :],
        }
        if moe_layer:
            w["router"] = _split_call(jnp.pad(moe_router[e], ((0, 0), (0, LANES - N_EXPERTS))))
        common = dict(n_rows=n_rows, n_blocks=n_prompt_blocks, blocks_per_seq=blocks_per_seq,
                      with_router=moe_layer)
        outs = _mixer_call(xp, w, **common)
        n_stream = 2 if moe_layer else 1
        cp, sp = outs[n_stream:]
        outs_s = _mixer_call(xs, w, sample_state=(cache_conv[l], state_gla[l]), into=outs[:n_stream], **common)
        cs, ss = outs_s[n_stream:]
        conv_p.append(cp)
        gla_p.append(sp)
        conv_s.append(cs)
        gla_s.append(ss)

        ln_g = ln_ffn_g[l][None, :]
        ln_b = ln_ffn_b[l][None, :]
        if moe_layer:
            x1, route = outs_s[:2]
            p1, p2, tile_expert, tile_valid, n_sorted = _moe_dispatch(route, n_rows)
            pos = jnp.stack([p1, p2]).reshape(2, n_prompt_blocks + 1, TB).transpose(1, 0, 2)
            ys = _ffn_moe_call(tile_expert, tile_valid, _dispatch_call(x1, pos, n_sorted),
                               moe_w_gate, moe_w_up, moe_w_down, e)
            comb = functools.partial(_combine_call, x1, _take_rows(ys, p1), _take_rows(ys, p2), route, ln_g, ln_b)
            if l == DEPTH - 1:
                xp = comb(first_block=0, n_blocks=n_prompt_blocks)
                xs = comb(first_block=n_prompt_blocks, n_blocks=1)
            else:
                xp = xs = comb(first_block=0, n_blocks=n_prompt_blocks + 1)
        else:
            x1 = outs_s[0]
            dense = functools.partial(_ffn_dense_call, x1, ffn_w_gate, ffn_w_up, ffn_w_down, e, ln_g, ln_b)
            x2 = dense(tm=TM_DENSE, first_block=0, n_blocks=n_prompt_rows // TM_DENSE, precise=False)
            xp = xs = dense(tm=TB, first_block=n_prompt_blocks, n_blocks=1, precise=(l == 0), into=x2)

    y_prompt = xp[:n_prompt_rows].reshape(n_prompt_seq, seq_len, D_MODEL)
    y_sample = xs[xs.shape[0] - TB:].reshape(n_sample, dec_len, D_MODEL)
    return (y_prompt, y_sample, jnp.stack(gla_p), jnp.stack(conv_p), jnp.stack(gla_s), jnp.stack(conv_s))
```

```python
import functools

import jax
import jax.numpy as jnp
from jax import lax
from jax.experimental import pallas as pl
from jax.experimental.pallas import tpu as pltpu

F32 = jnp.float32
BF16 = jnp.bfloat16

D_MODEL = 1024
DEPTH = 4
CHUNK = 64
CONV_DIM = 512
CONV_W = 3
GLA_HEADS = 4
GLA_HEAD_K = 64
GLA_HEAD_V = 128
GLA_QK = GLA_HEADS * GLA_HEAD_K
GLA_DV = GLA_HEADS * GLA_HEAD_V
GATE_RANK = 16
GATE_TEMP = 16.0
D_FF = 3584
N_EXPERTS = 8
ALPHA = (2 * DEPTH) ** 0.25
LN_EPS = 1e-5

LANES = 128
SUBLANES = 8
VMEM_LIMIT_BYTES = 56 * 1024 * 1024

COL_CB, COL_CC, COL_CH = 0, CONV_DIM, 2 * CONV_DIM
COL_Q = 3 * CONV_DIM
COL_K = COL_Q + GLA_QK
COL_V = COL_K + GLA_QK
COL_G = COL_V + GLA_DV
COL_A = COL_G + GLA_DV
IN_COLS_PAD = COL_A + LANES

TB = 512
SUB = 256
CHUNKS_PER_BLOCK = TB // CHUNK
TF = 512
TM_MOE = 1024
TM_DENSE = 1024
MOE_CALLS = 4
LANE_I1, LANE_I2, LANE_G1, LANE_G2 = 8, 9, 10, 11


def _dot(a, b):
    return jnp.dot(a, b, preferred_element_type=F32)


def _dot_tn(a, b):
    return lax.dot_general(a, b, (((0,), (0,)), ((), ())), preferred_element_type=F32)


def _dot_nt(a, b):
    return lax.dot_general(a, b, (((1,), (1,)), ((), ())), preferred_element_type=F32)


def _split_bf16(a):
    hi = a.astype(BF16)
    lo = (a - hi.astype(F32)).astype(BF16)
    return hi, lo


def _parts(a, precise):
    return _split_bf16(a) if precise else (a.astype(BF16),)


def _mm(a, b, dot=_dot):
    r = dot(a[0], b[0])
    if len(a) > 1:
        r = r + dot(a[1], b[0])
    if len(b) > 1:
        r = r + dot(a[0], b[1])
    return r


def _load(refs, idx=None):
    return tuple(r[...] if idx is None else r[idx] for r in refs)


def _store(refs, idx, parts):
    for r, p in zip(refs, parts):
        r[idx] = p


def _layer_norm(r, g, b):
    mu = jnp.mean(r, axis=-1, keepdims=True)
    d = r - mu
    var = jnp.mean(d * d, axis=-1, keepdims=True)
    return d * lax.rsqrt(var + LN_EPS) * g + b


def _silu(g):
    return g / (1.0 + jnp.exp(-g))


def _mixer_kernel(*refs, sample, precise, blocks_per_seq, with_router, n_alias):
    np_ = 2 if precise else 1
    it = iter(refs)

    def take(n):
        return [next(it) for _ in range(n)]

    x_ref = next(it)
    if sample:
        convin_ref = next(it)
        statein_ref = next(it)
    w_in_refs = take(np_)
    wa2_refs = take(np_)
    balpha_ref = next(it)
    convw_ref = next(it)
    gnorm_ref = next(it)
    w_out_refs = take(np_)
    lng_ref = next(it)
    lnb_ref = next(it)
    if with_router:
        rhi_ref = next(it)
        rlo_ref = next(it)
    take(n_alias)
    x1_ref = next(it)
    if with_router:
        x1b_ref = next(it)
        route_ref = next(it)
    convout_ref = next(it)
    stateout_ref = next(it)
    proj_ref = next(it)
    ue_ref = next(it)
    prev_ref = next(it)
    fx_ref = next(it)
    y_ref = next(it)
    o_ref = next(it)
    qg_refs = take(np_)
    kd_refs = take(np_)
    v_refs = take(np_)
    la_refs = take(2)
    s_ref = next(it)

    i = pl.program_id(0)
    seq_start = (i % blocks_per_seq) == 0

    proj_ref[...] = _mm(_parts(x_ref[...], precise), _load(w_in_refs))

    if sample:
        prev_ref[6:8, :] = convin_ref[0]
    else:
        @pl.when(seq_start)
        def _():
            prev_ref[...] = jnp.zeros_like(prev_ref)

    w0 = convw_ref[0:1, :]
    w1 = convw_ref[1:2, :]
    w2 = convw_ref[2:3, :]
    u = proj_ref[:, COL_CC:COL_CC + CONV_DIM] * proj_ref[:, COL_CH:COL_CH + CONV_DIM]
    ue_ref[0:8, :] = prev_ref[...]
    ue_ref[8:8 + TB, :] = u
    conv = ue_ref[6:6 + TB, :] * w0 + ue_ref[7:7 + TB, :] * w1 + u * w2
    y_ref[:, 0:CONV_DIM] = proj_ref[:, COL_CB:COL_CB + CONV_DIM] * conv
    if sample:
        for c in range(CHUNKS_PER_BLOCK):
            r0 = c * CHUNK
            if c > 0:
                fx_ref[c, 6:8, :] = convin_ref[c]
                fx_ref[c, 8:16, :] = ue_ref[8 + r0:16 + r0, :]
                convc = fx_ref[c, 6:14, :] * w0 + fx_ref[c, 7:15, :] * w1 + fx_ref[c, 8:16, :] * w2
                y_ref[r0:r0 + 8, 0:CONV_DIM] = proj_ref[r0:r0 + 8, COL_CB:COL_CB + CONV_DIM] * convc
            convout_ref[c] = ue_ref[8 + r0 + CHUNK - 2:8 + r0 + CHUNK, :]
    else:
        prev_ref[...] = ue_ref[TB:TB + 8, :]
        convout_ref[0] = ue_ref[TB + 6:TB + 8, :]

    z = _mm(_parts(proj_ref[:, COL_A:COL_A + LANES], precise), _load(wa2_refs)) + balpha_ref[...]
    la = (jnp.minimum(z, 0.0) - jnp.log(1.0 + jnp.exp(-jnp.abs(z)))) * (1.0 / GATE_TEMP)
    _store(la_refs, slice(None), _split_bf16(la))

    row = lax.broadcasted_iota(jnp.int32, (SUB, SUB), 0)
    col = lax.broadcasted_iota(jnp.int32, (SUB, SUB), 1)
    same_chunk = (row // CHUNK) == (col // CHUNK)
    causal = jnp.logical_and(same_chunk, row >= col)
    tril_bd = (causal.astype(BF16),)
    ones_bd = (same_chunk.astype(BF16),)
    lane_head = lax.broadcasted_iota(jnp.int32, (SUB, GLA_QK), 1) // GLA_HEAD_K

    for sb in range(TB // SUB):
        rs = slice(sb * SUB, (sb + 1) * SUB)
        lap = _load(la_refs, (rs, slice(None)))
        bcum = _mm(tril_bd, lap)
        blast = _mm(ones_bd, lap)
        q = proj_ref[rs, COL_Q:COL_Q + GLA_QK]
        k = proj_ref[rs, COL_K:COL_K + GLA_QK]
        qg = q * jnp.exp(bcum) * (GLA_HEAD_K ** -0.5)
        kg = _parts(k * jnp.exp(-bcum), precise)
        _store(qg_refs, (rs, slice(None)), _parts(qg, precise))
        _store(kd_refs, (rs, slice(None)), _parts(k * jnp.exp(blast - bcum), precise))
        vp = _parts(proj_ref[rs, COL_V:COL_V + GLA_DV], precise)
        _store(v_refs, (rs, slice(None)), vp)
        for h in range(GLA_HEADS):
            vs = slice(h * GLA_HEAD_V, (h + 1) * GLA_HEAD_V)
            qm = _parts(jnp.where(lane_head == h, qg, 0.0), precise)
            sc = jnp.where(causal, _mm(qm, kg, _dot_nt), 0.0)
            o_ref[rs, vs] = _mm(_parts(sc, precise), tuple(p[:, vs] for p in vp))

    ones_cv = (jnp.ones((CHUNK, GLA_HEAD_V), BF16),)
    bd_row = lax.broadcasted_iota(jnp.int32, (GLA_QK, GLA_DV), 0) // GLA_HEAD_K
    bd_col = lax.broadcasted_iota(jnp.int32, (GLA_QK, GLA_DV), 1) // GLA_HEAD_V
    head_diag = bd_row == bd_col
    if not sample:
        s = jnp.where(seq_start, 0.0, s_ref[...])
    for c in range(CHUNKS_PER_BLOCK):
        cs = (slice(c * CHUNK, (c + 1) * CHUNK), slice(None))
        if sample:
            s = statein_ref[c]
        s_rows = s.reshape(GLA_QK, GLA_HEAD_V)
        s_bd = jnp.where(head_diag, jnp.concatenate([s_rows] * GLA_HEADS, axis=1), 0.0)
        o_ref[cs] += _mm(_load(qg_refs, cs), _parts(s_bd, precise))
        ds = _mm(_load(kd_refs, cs), _load(v_refs, cs), _dot_tn)
        dec = jnp.exp(_mm(_load(la_refs, cs), ones_cv, _dot_tn))
        ds_diag = jnp.stack([ds[h * GLA_HEAD_K:(h + 1) * GLA_HEAD_K, h * GLA_HEAD_V:(h + 1) * GLA_HEAD_V]
                             for h in range(GLA_HEADS)])
        s = dec.reshape(GLA_HEADS, GLA_HEAD_K, GLA_HEAD_V) * s + ds_diag
        if sample:
            stateout_ref[c] = s
    if not sample:
        s_ref[...] = s
        stateout_ref[0] = s

    gn = gnorm_ref[...]
    for h in range(GLA_HEADS):
        vs = slice(h * GLA_HEAD_V, (h + 1) * GLA_HEAD_V)
        oh = o_ref[:, vs]
        ms = jnp.mean(oh * oh, axis=-1, keepdims=True)
        g = proj_ref[:, COL_G + h * GLA_HEAD_V:COL_G + (h + 1) * GLA_HEAD_V]
        y_ref[:, CONV_DIM + h * GLA_HEAD_V:CONV_DIM + (h + 1) * GLA_HEAD_V] = (
            oh * lax.rsqrt(ms + LN_EPS) * gn * _silu(g))

    m = _mm(_parts(y_ref[...], precise), _load(w_out_refs))
    x1 = _layer_norm(ALPHA * x_ref[...] + m, lng_ref[...], lnb_ref[...])
    x1_ref[...] = x1

    if with_router:
        x1b_ref[...] = x1.astype(BF16)
        lg = _mm(_split_bf16(x1), (rhi_ref[...], rlo_ref[...]))
        lane = lax.broadcasted_iota(jnp.int32, lg.shape, 1)
        neg = jnp.float32(-jnp.inf)
        lg = jnp.where(lane < N_EXPERTS, lg, neg)
        m1 = jnp.max(lg, axis=-1, keepdims=True)
        i1 = jnp.min(jnp.where(lg == m1, lane, LANES), axis=-1, keepdims=True)
        lg2 = jnp.where(lane == i1, neg, lg)
        m2 = jnp.max(lg2, axis=-1, keepdims=True)
        i2 = jnp.min(jnp.where(lg2 == m2, lane, LANES), axis=-1, keepdims=True)
        e2 = jnp.exp(m2 - m1)
        g1 = 1.0 / (1.0 + e2)
        g2 = e2 / (1.0 + e2)
        slab = jnp.where(lane == i1, g1, jnp.where(lane == i2, g2, 0.0))
        slab = jnp.where(lane == LANE_I1, i1.astype(F32), slab)
        slab = jnp.where(lane == LANE_I2, i2.astype(F32), slab)
        slab = jnp.where(lane == LANE_G1, g1, slab)
        slab = jnp.where(lane == LANE_G2, g2, slab)
        route_ref[...] = slab


def _mixer_call(x, w, *, n_rows, n_blocks, blocks_per_seq, with_router, sample_state=None, into=None):
    sample = sample_state is not None
    precise = sample
    np_ = 2 if precise else 1
    const2 = lambda i: (0, 0)
    const3 = lambda i: (0, 0, 0)
    const4 = lambda i: (0, 0, 0, 0)

    x_last = x.shape[0] // TB - 1
    in_specs = [pl.BlockSpec((TB, D_MODEL), (lambda i: (x_last, 0)) if sample else (lambda i: (i, 0)))]
    args = [x]
    if sample:
        conv_in, state_in = sample_state
        in_specs += [pl.BlockSpec(conv_in.shape, const3), pl.BlockSpec(state_in.shape, const4)]
        args += [conv_in, state_in]
    in_specs += [pl.BlockSpec((D_MODEL, IN_COLS_PAD), const2)] * np_
    args += list(w["w_in"][:np_])
    in_specs += [pl.BlockSpec((LANES, GLA_QK), const2)] * np_
    args += list(w["wa2"][:np_])
    in_specs += [pl.BlockSpec((1, GLA_QK), const2), pl.BlockSpec((CONV_W, CONV_DIM), const2),
                 pl.BlockSpec((1, GLA_HEAD_V), const2)]
    args += [w["b_alpha"], w["conv_w"], w["gla_norm_g"]]
    in_specs += [pl.BlockSpec((D_MODEL, D_MODEL), const2)] * np_
    args += list(w["w_out"][:np_])
    in_specs += [pl.BlockSpec((1, D_MODEL), const2)] * 2
    args += [w["ln_g"], w["ln_b"]]
    if with_router:
        in_specs += [pl.BlockSpec((D_MODEL, LANES), const2)] * 2
        args += list(w["router"])
    aliases = {}
    n_alias = 0
    if sample:
        n_alias = len(into)
        for k, buf in enumerate(into):
            aliases[len(args)] = k
            in_specs.append(pl.BlockSpec(memory_space=pl.ANY))
            args.append(buf)

    out_row = (lambda i: (n_blocks, 0)) if sample else (lambda i: (i, 0))
    out_shape = [jax.ShapeDtypeStruct((n_rows, D_MODEL), F32)]
    out_specs = [pl.BlockSpec((TB, D_MODEL), out_row)]
    if with_router:
        out_shape += [jax.ShapeDtypeStruct((n_rows, D_MODEL), BF16),
                      jax.ShapeDtypeStruct((n_rows, LANES), F32)]
        out_specs += [pl.BlockSpec((TB, D_MODEL), out_row), pl.BlockSpec((TB, LANES), out_row)]
    if sample:
        out_shape += [jax.ShapeDtypeStruct(conv_in.shape, F32), jax.ShapeDtypeStruct(state_in.shape, F32)]
        out_specs += [pl.BlockSpec(conv_in.shape, const3), pl.BlockSpec(state_in.shape, const4)]
    else:
        n_seq = n_blocks // blocks_per_seq
        out_shape += [jax.ShapeDtypeStruct((n_seq, CONV_W - 1, CONV_DIM), F32),
                      jax.ShapeDtypeStruct((n_seq, GLA_HEADS, GLA_HEAD_K, GLA_HEAD_V), F32)]
        out_specs += [pl.BlockSpec((1, CONV_W - 1, CONV_DIM), lambda i: (i // blocks_per_seq, 0, 0)),
                      pl.BlockSpec((1, GLA_HEADS, GLA_HEAD_K, GLA_HEAD_V),
                                   lambda i: (i // blocks_per_seq, 0, 0, 0))]
    scratch = [
        pltpu.VMEM((TB, IN_COLS_PAD), F32),
        pltpu.VMEM((TB + 8, CONV_DIM), F32),
        pltpu.VMEM((8, CONV_DIM), F32),
        pltpu.VMEM((CHUNKS_PER_BLOCK, 16, CONV_DIM), F32),
        pltpu.VMEM((TB, D_MODEL), F32),
        pltpu.VMEM((TB, GLA_DV), F32),
    ]
    scratch += [pltpu.VMEM((TB, GLA_QK), BF16)] * np_
    scratch += [pltpu.VMEM((TB, GLA_QK), BF16)] * np_
    scratch += [pltpu.VMEM((TB, GLA_DV), BF16)] * np_
    scratch += [pltpu.VMEM((TB, GLA_QK), BF16)] * 2
    scratch += [pltpu.VMEM((GLA_HEADS, GLA_HEAD_K, GLA_HEAD_V), F32)]
    kern = functools.partial(_mixer_kernel, sample=sample, precise=precise, blocks_per_seq=blocks_per_seq,
                             with_router=with_router, n_alias=n_alias)
    name = ("mixer_sample" if sample else "mixer_prompt") + ("_router" if with_router else "")
    return pl.pallas_call(
        kern,
        grid=(1 if sample else n_blocks,),
        in_specs=in_specs,
        out_specs=out_specs,
        out_shape=out_shape,
        scratch_shapes=scratch,
        input_output_aliases=aliases,
        compiler_params=pltpu.CompilerParams(dimension_semantics=("arbitrary",),
                                             vmem_limit_bytes=VMEM_LIMIT_BYTES),
        name=name,
    )(*args)


def _swiglu_step(xp, wg, wu, wd, acc_ref, precise):
    f = pl.program_id(1)
    g = _mm(xp, _parts(wg, precise))
    u = _mm(xp, _parts(wu, precise))
    part = _mm(_parts(_silu(g) * u, precise), _parts(wd, precise))

    @pl.when(f == 0)
    def _():
        acc_ref[...] = part

    @pl.when(f > 0)
    def _():
        acc_ref[...] += part


def _ffn_dense_kernel(*refs, precise, n_alias):
    x_ref, wg_ref, wu_ref, wd_ref, lng_ref, lnb_ref = refs[:6]
    out_ref, acc_ref = refs[6 + n_alias:]
    x = x_ref[...]
    _swiglu_step(_parts(x, precise), wg_ref[0], wu_ref[0], wd_ref[0], acc_ref, precise)

    @pl.when(pl.program_id(1) == pl.num_programs(1) - 1)
    def _():
        out_ref[...] = _layer_norm(ALPHA * x + acc_ref[...], lng_ref[...], lnb_ref[...])


def _ffn_dense_call(x, wg, wu, wd, e, ln_g, ln_b, *, tm, first_block, n_blocks, precise, into=None):
    n_rows = x.shape[0]
    nf = D_FF // TF
    row_spec = pl.BlockSpec((tm, D_MODEL), lambda t, f: (first_block + t, 0))
    in_specs = [row_spec,
                pl.BlockSpec((1, D_MODEL, TF), lambda t, f: (e, 0, f)),
                pl.BlockSpec((1, D_MODEL, TF), lambda t, f: (e, 0, f)),
                pl.BlockSpec((1, TF, D_MODEL), lambda t, f: (e, f, 0)),
                pl.BlockSpec((1, D_MODEL), lambda t, f: (0, 0)),
                pl.BlockSpec((1, D_MODEL), lambda t, f: (0, 0))]
    args = [x, wg, wu, wd, ln_g, ln_b]
    aliases = {}
    n_alias = 0
    if into is not None:
        n_alias = 1
        aliases[len(args)] = 0
        in_specs.append(pl.BlockSpec(memory_space=pl.ANY))
        args.append(into)
    return pl.pallas_call(
        functools.partial(_ffn_dense_kernel, precise=precise, n_alias=n_alias),
        grid=(n_blocks, nf),
        in_specs=in_specs,
        out_specs=row_spec,
        out_shape=jax.ShapeDtypeStruct((n_rows, D_MODEL), F32),
        scratch_shapes=[pltpu.VMEM((tm, D_MODEL), F32)],
        input_output_aliases=aliases,
        compiler_params=pltpu.CompilerParams(dimension_semantics=("arbitrary", "arbitrary"),
                                             vmem_limit_bytes=VMEM_LIMIT_BYTES),
        name="ffn_dense_precise" if precise else "ffn_dense",
    )(*args)


def _ffn_moe_kernel(te_ref, nt_ref, x_ref, wg_ref, wu_ref, wd_ref, *rest, first_tile, n_alias):
    out_ref, acc_ref = rest[n_alias:]
    used = first_tile + pl.program_id(0) < nt_ref[0]

    @pl.when(used)
    def _():
        _swiglu_step((x_ref[...],), wg_ref[0, 0], wu_ref[0, 0], wd_ref[0, 0], acc_ref, False)

    last = pl.program_id(1) == pl.num_programs(1) - 1

    @pl.when(jnp.logical_and(last, used))
    def _():
        out_ref[...] = acc_ref[...]

    @pl.when(jnp.logical_and(last, jnp.logical_not(used)))
    def _():
        out_ref[...] = jnp.zeros_like(out_ref)


def _ffn_moe_call(tile_expert, n_tiles_used, xs, wg, wu, wd, e, *, first_tile, n_sorted, into=None):
    nf = D_FF // TF
    in_specs = [
        pl.BlockSpec((TM_MOE, D_MODEL), lambda t, f, te, nt: (t, 0)),
        pl.BlockSpec((1, 1, D_MODEL, TF), lambda t, f, te, nt: (e, te[first_tile + t], 0, f)),
        pl.BlockSpec((1, 1, D_MODEL, TF), lambda t, f, te, nt: (e, te[first_tile + t], 0, f)),
        pl.BlockSpec((1, 1, TF, D_MODEL), lambda t, f, te, nt: (e, te[first_tile + t], f, 0)),
    ]
    args = [tile_expert, n_tiles_used, xs, wg, wu, wd]
    aliases = {}
    n_alias = 0
    if into is not None:
        n_alias = 1
        aliases[len(args)] = 0
        in_specs.append(pl.BlockSpec(memory_space=pl.ANY))
        args.append(into)
    grid_spec = pltpu.PrefetchScalarGridSpec(
        num_scalar_prefetch=2,
        grid=(xs.shape[0] // TM_MOE, nf),
        in_specs=in_specs,
        out_specs=pl.BlockSpec((TM_MOE, D_MODEL), lambda t, f, te, nt: (first_tile + t, 0)),
        scratch_shapes=[pltpu.VMEM((TM_MOE, D_MODEL), F32)],
    )
    return pl.pallas_call(
        functools.partial(_ffn_moe_kernel, first_tile=first_tile, n_alias=n_alias),
        grid_spec=grid_spec,
        out_shape=jax.ShapeDtypeStruct((n_sorted, D_MODEL), F32),
        input_output_aliases=aliases,
        compiler_params=pltpu.CompilerParams(dimension_semantics=("arbitrary", "arbitrary"),
                                             vmem_limit_bytes=VMEM_LIMIT_BYTES),
        name="ffn_moe",
    )(*args)


def _combine_kernel(x_ref, ya_ref, yb_ref, route_ref, lng_ref, lnb_ref, out_ref):
    slab = route_ref[...]
    lane = lax.broadcasted_iota(jnp.int32, slab.shape, 1)
    g1 = jnp.sum(jnp.where(lane == LANE_G1, slab, 0.0), axis=-1, keepdims=True)
    g2 = jnp.sum(jnp.where(lane == LANE_G2, slab, 0.0), axis=-1, keepdims=True)
    r = ALPHA * x_ref[...] + (g1 * ya_ref[...] + g2 * yb_ref[...])
    out_ref[...] = _layer_norm(r, lng_ref[...], lnb_ref[...])


def _combine_call(x, ya, yb, route, ln_g, ln_b, *, first_block, n_blocks):
    in_spec = pl.BlockSpec((TB, D_MODEL), lambda t: (first_block + t, 0))
    route_spec = pl.BlockSpec((TB, LANES), lambda t: (first_block + t, 0))
    vec_spec = pl.BlockSpec((1, D_MODEL), lambda t: (0, 0))
    return pl.pallas_call(
        _combine_kernel,
        grid=(n_blocks,),
        in_specs=[in_spec, in_spec, in_spec, route_spec, vec_spec, vec_spec],
        out_specs=pl.BlockSpec((TB, D_MODEL), lambda t: (t, 0)),
        out_shape=jax.ShapeDtypeStruct((n_blocks * TB, D_MODEL), F32),
        compiler_params=pltpu.CompilerParams(dimension_semantics=("arbitrary",),
                                             vmem_limit_bytes=VMEM_LIMIT_BYTES),
        name="moe_combine",
    )(x, ya, yb, route, ln_g, ln_b)


def _take_rows(a, idx):
    return a.at[idx].get(mode="promise_in_bounds")


def _moe_dispatch(route, n_rows):
    i1 = route[:, LANE_I1].astype(jnp.int32)
    i2 = route[:, LANE_I2].astype(jnp.int32)
    experts = jnp.arange(N_EXPERTS, dtype=jnp.int32)
    hit = ((i1[:, None] == experts) | (i2[:, None] == experts)).astype(jnp.int32)
    before = jnp.cumsum(hit, axis=0) - hit
    counts = jnp.sum(hit, axis=0)
    padded = ((counts + TM_MOE - 1) // TM_MOE) * TM_MOE
    ends = jnp.cumsum(padded)
    starts = ends - padded
    rank1 = jnp.sum(jnp.where(i1[:, None] == experts, before, 0), axis=1)
    rank2 = jnp.sum(jnp.where(i2[:, None] == experts, before, 0), axis=1)
    p1 = _take_rows(starts, i1) + rank1
    p2 = _take_rows(starts, i2) + rank2
    n_tiles = (2 * n_rows + N_EXPERTS * (TM_MOE - 1) + TM_MOE - 1) // TM_MOE
    tok = jnp.arange(n_rows, dtype=jnp.int32)
    row_src = jnp.zeros((n_tiles * TM_MOE,), jnp.int32).at[jnp.concatenate([p1, p2])].set(
        jnp.concatenate([tok, tok]), mode="promise_in_bounds", unique_indices=True)
    tile_start = jnp.arange(n_tiles, dtype=jnp.int32) * TM_MOE
    tile_expert = jnp.minimum(jnp.sum(tile_start[:, None] >= ends[None, :], axis=1), N_EXPERTS - 1)
    n_tiles_used = (ends[-1] // TM_MOE).reshape(1)
    return row_src, p1, p2, tile_expert.astype(jnp.int32), n_tiles_used.astype(jnp.int32)


def _split_kernel(a_ref, hi_ref, lo_ref):
    hi, lo = _split_bf16(a_ref[...])
    hi_ref[...] = hi
    lo_ref[...] = lo


def _split_call(a):
    rows, cols = a.shape
    tr = min(rows, 256)
    spec = pl.BlockSpec((tr, cols), lambda r: (r, 0))
    return pl.pallas_call(
        _split_kernel,
        grid=(rows // tr,),
        in_specs=[spec],
        out_specs=[spec, spec],
        out_shape=[jax.ShapeDtypeStruct(a.shape, BF16)] * 2,
        name="split_bf16",
    )(a)


def _pad_w_in(w_in):
    main = w_in[:, :COL_A]
    alr = jnp.pad(w_in[:, COL_A:], ((0, 0), (0, LANES - GATE_RANK)))
    return jnp.concatenate([main, alr], axis=1)


def kernel(x_prompt, x_sample, cache_conv, state_gla, ln_mix_g, ln_mix_b, w_in, w_alpha2, b_alpha,
           conv_w, gla_norm_g, w_out, ln_ffn_g, ln_ffn_b, ffn_w_gate, ffn_w_up, ffn_w_down,
           moe_router, moe_w_gate, moe_w_up, moe_w_down):
    n_prompt_seq, seq_len, _ = x_prompt.shape
    n_sample, dec_len, _ = x_sample.shape
    assert dec_len == CHUNK and n_sample * dec_len == TB and seq_len % TB == 0
    n_prompt_rows = n_prompt_seq * seq_len
    n_rows = n_prompt_rows + TB
    blocks_per_seq = seq_len // TB
    n_prompt_blocks = n_prompt_rows // TB
    assert n_prompt_rows % TM_DENSE == 0

    xp = x_prompt.reshape(n_prompt_rows, D_MODEL)
    xs = x_sample.reshape(TB, D_MODEL)
    gla_p, conv_p, gla_s, conv_s = [], [], [], []
    for l in range(DEPTH):
        moe_layer = l % 2 == 1
        e = l // 2
        w = {
            "w_in": _split_call(_pad_w_in(w_in[l])),
            "wa2": _split_call(jnp.pad(w_alpha2[l], ((0, LANES - GATE_RANK), (0, 0)))),
            "b_alpha": b_alpha[l][None, :],
            "conv_w": conv_w[l],
            "gla_norm_g": gla_norm_g[l][None, :],
            "w_out": _split_call(w_out[l]),
            "ln_g": ln_mix_g[l][None, :],
            "ln_b": ln_mix_b[l][None, :],
        }
        if moe_layer:
            w["router"] = _split_call(jnp.pad(moe_router[e], ((0, 0), (0, LANES - N_EXPERTS))))
        common = dict(n_rows=n_rows, n_blocks=n_prompt_blocks, blocks_per_seq=blocks_per_seq,
                      with_router=moe_layer)
        outs = _mixer_call(xp, w, **common)
        n_stream = 3 if moe_layer else 1
        cp, sp = outs[n_stream:]
        outs_s = _mixer_call(xs, w, sample_state=(cache_conv[l], state_gla[l]), into=outs[:n_stream], **common)
        cs, ss = outs_s[n_stream:]
        conv_p.append(cp)
        gla_p.append(sp)
        conv_s.append(cs)
        gla_s.append(ss)

        ln_g = ln_ffn_g[l][None, :]
        ln_b = ln_ffn_b[l][None, :]
        if moe_layer:
            x1, x1b, route = outs_s[:3]
            row_src, p1, p2, tile_expert, n_tiles_used = _moe_dispatch(route, n_rows)
            n_tiles = row_src.shape[0] // TM_MOE
            bounds = [n_tiles * c // MOE_CALLS for c in range(MOE_CALLS + 1)]
            ys = None
            for t0, t1 in zip(bounds[:-1], bounds[1:]):
                xsort = _take_rows(x1b, row_src[t0 * TM_MOE:t1 * TM_MOE])
                ys = _ffn_moe_call(tile_expert, n_tiles_used, xsort, moe_w_gate, moe_w_up, moe_w_down, e,
                                   first_tile=t0, n_sorted=row_src.shape[0], into=ys)
            comb = functools.partial(_combine_call, x1, _take_rows(ys, p1), _take_rows(ys, p2), route, ln_g, ln_b)
            if l == DEPTH - 1:
                xp = comb(first_block=0, n_blocks=n_prompt_blocks)
                xs = comb(first_block=n_prompt_blocks, n_blocks=1)
            else:
                xp = xs = comb(first_block=0, n_blocks=n_prompt_blocks + 1)
        else:
            x1 = outs_s[0]
            dense = functools.partial(_ffn_dense_call, x1, ffn_w_gate, ffn_w_up, ffn_w_down, e, ln_g, ln_b)
            x2 = dense(tm=TM_DENSE, first_block=0, n_blocks=n_prompt_rows // TM_DENSE, precise=False)
            xp = xs = dense(tm=TB, first_block=n_prompt_blocks, n_blocks=1, precise=(l == 0), into=x2)

    y_prompt = xp[:n_prompt_rows].reshape(n_prompt_seq, seq_len, D_MODEL)
    y_sample = xs[xs.shape[0] - TB:].reshape(n_sample, dec_len, D_MODEL)
    return (y_prompt, y_sample, jnp.stack(gla_p), jnp.stack(conv_p), jnp.stack(gla_s), jnp.stack(conv_s))
```

```python
import functools

import jax
import jax.numpy as jnp
from jax import lax
from jax.experimental import pallas as pl
from jax.experimental.pallas import tpu as pltpu

F32 = jnp.float32
BF16 = jnp.bfloat16

D_MODEL = 1024
DEPTH = 4
CHUNK = 64
CONV_DIM = 512
CONV_W = 3
GLA_HEADS = 4
GLA_HEAD_K = 64
GLA_HEAD_V = 128
GLA_QK = GLA_HEADS * GLA_HEAD_K
GLA_DV = GLA_HEADS * GLA_HEAD_V
GATE_RANK = 16
GATE_TEMP = 16.0
D_FF = 3584
N_EXPERTS = 8
ALPHA = (2 * DEPTH) ** 0.25
LN_EPS = 1e-5

LANES = 128
SUBLANES = 8
VMEM_LIMIT_BYTES = 56 * 1024 * 1024

COL_CB, COL_CC, COL_CH = 0, CONV_DIM, 2 * CONV_DIM
COL_Q = 3 * CONV_DIM
COL_K = COL_Q + GLA_QK
COL_V = COL_K + GLA_QK
COL_G = COL_V + GLA_DV
COL_A = COL_G + GLA_DV
IN_COLS_PAD = COL_A + LANES

TB = 512
SUB = 256
CHUNKS_PER_BLOCK = TB // CHUNK
TF = 512
N_FF_SLICES = D_FF // TF
TM_MOE = 1024
TM_DENSE = 1024
MOE_CALLS = 4
LANE_I1, LANE_I2, LANE_G1, LANE_G2 = 8, 9, 10, 11


def _dot(a, b):
    return jnp.dot(a, b, preferred_element_type=F32)


def _dot_tn(a, b):
    return lax.dot_general(a, b, (((0,), (0,)), ((), ())), preferred_element_type=F32)


def _dot_nt(a, b):
    return lax.dot_general(a, b, (((1,), (1,)), ((), ())), preferred_element_type=F32)


def _split_bf16(a):
    hi = a.astype(BF16)
    lo = (a - hi.astype(F32)).astype(BF16)
    return hi, lo


def _parts(a, precise):
    return _split_bf16(a) if precise else (a.astype(BF16),)


def _mm(a, b, dot=_dot):
    r = dot(a[0], b[0])
    if len(a) > 1:
        r = r + dot(a[1], b[0])
    if len(b) > 1:
        r = r + dot(a[0], b[1])
    return r


def _load(refs, idx=None):
    return tuple(r[...] if idx is None else r[idx] for r in refs)


def _store(refs, idx, parts):
    for r, p in zip(refs, parts):
        r[idx] = p


def _layer_norm(r, g, b):
    mu = jnp.mean(r, axis=-1, keepdims=True)
    d = r - mu
    var = jnp.mean(d * d, axis=-1, keepdims=True)
    return d * lax.rsqrt(var + LN_EPS) * g + b


def _silu(g):
    return g / (1.0 + jnp.exp(-g))


def _mixer_kernel(*refs, sample, precise, blocks_per_seq, with_router, n_alias):
    np_ = 2 if precise else 1
    it = iter(refs)

    def take(n):
        return [next(it) for _ in range(n)]

    x_ref = next(it)
    if sample:
        convin_ref = next(it)
        statein_ref = next(it)
    w_in_refs = take(np_)
    wa2_refs = take(np_)
    balpha_ref = next(it)
    convw_ref = next(it)
    gnorm_ref = next(it)
    w_out_refs = take(np_)
    lng_ref = next(it)
    lnb_ref = next(it)
    if with_router:
        rhi_ref = next(it)
        rlo_ref = next(it)
    take(n_alias)
    x1_ref = next(it)
    if with_router:
        x1b_ref = next(it)
        route_ref = next(it)
    convout_ref = next(it)
    stateout_ref = next(it)
    proj_ref = next(it)
    ue_ref = next(it)
    prev_ref = next(it)
    fx_ref = next(it)
    y_ref = next(it)
    o_ref = next(it)
    qg_refs = take(np_)
    kd_refs = take(np_)
    v_refs = take(np_)
    la_refs = take(2)
    s_ref = next(it)

    i = pl.program_id(0)
    seq_start = (i % blocks_per_seq) == 0

    proj_ref[...] = _mm(_parts(x_ref[...], precise), _load(w_in_refs))

    if sample:
        prev_ref[6:8, :] = convin_ref[0]
    else:
        @pl.when(seq_start)
        def _():
            prev_ref[...] = jnp.zeros_like(prev_ref)

    w0 = convw_ref[0:1, :]
    w1 = convw_ref[1:2, :]
    w2 = convw_ref[2:3, :]
    u = proj_ref[:, COL_CC:COL_CC + CONV_DIM] * proj_ref[:, COL_CH:COL_CH + CONV_DIM]
    ue_ref[0:8, :] = prev_ref[...]
    ue_ref[8:8 + TB, :] = u
    conv = ue_ref[6:6 + TB, :] * w0 + ue_ref[7:7 + TB, :] * w1 + u * w2
    y_ref[:, 0:CONV_DIM] = proj_ref[:, COL_CB:COL_CB + CONV_DIM] * conv
    if sample:
        for c in range(CHUNKS_PER_BLOCK):
            r0 = c * CHUNK
            if c > 0:
                fx_ref[c, 6:8, :] = convin_ref[c]
                fx_ref[c, 8:16, :] = ue_ref[8 + r0:16 + r0, :]
                convc = fx_ref[c, 6:14, :] * w0 + fx_ref[c, 7:15, :] * w1 + fx_ref[c, 8:16, :] * w2
                y_ref[r0:r0 + 8, 0:CONV_DIM] = proj_ref[r0:r0 + 8, COL_CB:COL_CB + CONV_DIM] * convc
            convout_ref[c] = ue_ref[8 + r0 + CHUNK - 2:8 + r0 + CHUNK, :]
    else:
        prev_ref[...] = ue_ref[TB:TB + 8, :]
        convout_ref[0] = ue_ref[TB + 6:TB + 8, :]

    z = _mm(_parts(proj_ref[:, COL_A:COL_A + LANES], precise), _load(wa2_refs)) + balpha_ref[...]
    la = (jnp.minimum(z, 0.0) - jnp.log(1.0 + jnp.exp(-jnp.abs(z)))) * (1.0 / GATE_TEMP)
    _store(la_refs, slice(None), _split_bf16(la))

    row = lax.broadcasted_iota(jnp.int32, (SUB, SUB), 0)
    col = lax.broadcasted_iota(jnp.int32, (SUB, SUB), 1)
    same_chunk = (row // CHUNK) == (col // CHUNK)
    causal = jnp.logical_and(same_chunk, row >= col)
    tril_bd = (causal.astype(BF16),)
    ones_bd = (same_chunk.astype(BF16),)
    lane_head = lax.broadcasted_iota(jnp.int32, (SUB, GLA_QK), 1) // GLA_HEAD_K

    for sb in range(TB // SUB):
        rs = slice(sb * SUB, (sb + 1) * SUB)
        lap = _load(la_refs, (rs, slice(None)))
        bcum = _mm(tril_bd, lap)
        blast = _mm(ones_bd, lap)
        q = proj_ref[rs, COL_Q:COL_Q + GLA_QK]
        k = proj_ref[rs, COL_K:COL_K + GLA_QK]
        qg = q * jnp.exp(bcum) * (GLA_HEAD_K ** -0.5)
        kg = _parts(k * jnp.exp(-bcum), precise)
        _store(qg_refs, (rs, slice(None)), _parts(qg, precise))
        _store(kd_refs, (rs, slice(None)), _parts(k * jnp.exp(blast - bcum), precise))
        vp = _parts(proj_ref[rs, COL_V:COL_V + GLA_DV], precise)
        _store(v_refs, (rs, slice(None)), vp)
        for h in range(GLA_HEADS):
            vs = slice(h * GLA_HEAD_V, (h + 1) * GLA_HEAD_V)
            qm = _parts(jnp.where(lane_head == h, qg, 0.0), precise)
            sc = jnp.where(causal, _mm(qm, kg, _dot_nt), 0.0)
            o_ref[rs, vs] = _mm(_parts(sc, precise), tuple(p[:, vs] for p in vp))

    ones_cv = (jnp.ones((CHUNK, GLA_HEAD_V), BF16),)
    bd_row = lax.broadcasted_iota(jnp.int32, (GLA_QK, GLA_DV), 0) // GLA_HEAD_K
    bd_col = lax.broadcasted_iota(jnp.int32, (GLA_QK, GLA_DV), 1) // GLA_HEAD_V
    head_diag = bd_row == bd_col
    if not sample:
        s = jnp.where(seq_start, 0.0, s_ref[...])
    for c in range(CHUNKS_PER_BLOCK):
        cs = (slice(c * CHUNK, (c + 1) * CHUNK), slice(None))
        if sample:
            s = statein_ref[c]
        s_rows = s.reshape(GLA_QK, GLA_HEAD_V)
        s_bd = jnp.where(head_diag, jnp.concatenate([s_rows] * GLA_HEADS, axis=1), 0.0)
        o_ref[cs] += _mm(_load(qg_refs, cs), _parts(s_bd, precise))
        ds = _mm(_load(kd_refs, cs), _load(v_refs, cs), _dot_tn)
        dec = jnp.exp(_mm(_load(la_refs, cs), ones_cv, _dot_tn))
        ds_diag = jnp.stack([ds[h * GLA_HEAD_K:(h + 1) * GLA_HEAD_K, h * GLA_HEAD_V:(h + 1) * GLA_HEAD_V]
                             for h in range(GLA_HEADS)])
        s = dec.reshape(GLA_HEADS, GLA_HEAD_K, GLA_HEAD_V) * s + ds_diag
        if sample:
            stateout_ref[c] = s
    if not sample:
        s_ref[...] = s
        stateout_ref[0] = s

    gn = gnorm_ref[...]
    for h in range(GLA_HEADS):
        vs = slice(h * GLA_HEAD_V, (h + 1) * GLA_HEAD_V)
        oh = o_ref[:, vs]
        ms = jnp.mean(oh * oh, axis=-1, keepdims=True)
        g = proj_ref[:, COL_G + h * GLA_HEAD_V:COL_G + (h + 1) * GLA_HEAD_V]
        y_ref[:, CONV_DIM + h * GLA_HEAD_V:CONV_DIM + (h + 1) * GLA_HEAD_V] = (
            oh * lax.rsqrt(ms + LN_EPS) * gn * _silu(g))

    m = _mm(_parts(y_ref[...], precise), _load(w_out_refs))
    x1 = _layer_norm(ALPHA * x_ref[...] + m, lng_ref[...], lnb_ref[...])
    x1_ref[...] = x1

    if with_router:
        x1b_ref[...] = x1.astype(BF16)
        lg = _mm(_split_bf16(x1), (rhi_ref[...], rlo_ref[...]))
        lane = lax.broadcasted_iota(jnp.int32, lg.shape, 1)
        neg = jnp.float32(-jnp.inf)
        lg = jnp.where(lane < N_EXPERTS, lg, neg)
        m1 = jnp.max(lg, axis=-1, keepdims=True)
        i1 = jnp.min(jnp.where(lg == m1, lane, LANES), axis=-1, keepdims=True)
        lg2 = jnp.where(lane == i1, neg, lg)
        m2 = jnp.max(lg2, axis=-1, keepdims=True)
        i2 = jnp.min(jnp.where(lg2 == m2, lane, LANES), axis=-1, keepdims=True)
        e2 = jnp.exp(m2 - m1)
        g1 = 1.0 / (1.0 + e2)
        g2 = e2 / (1.0 + e2)
        slab = jnp.where(lane == i1, g1, jnp.where(lane == i2, g2, 0.0))
        slab = jnp.where(lane == LANE_I1, i1.astype(F32), slab)
        slab = jnp.where(lane == LANE_I2, i2.astype(F32), slab)
        slab = jnp.where(lane == LANE_G1, g1, slab)
        slab = jnp.where(lane == LANE_G2, g2, slab)
        route_ref[...] = slab


def _mixer_call(x, w, *, n_rows, n_blocks, blocks_per_seq, with_router, sample_state=None, into=None):
    sample = sample_state is not None
    precise = sample
    np_ = 2 if precise else 1
    const2 = lambda i: (0, 0)
    const3 = lambda i: (0, 0, 0)
    const4 = lambda i: (0, 0, 0, 0)

    x_last = x.shape[0] // TB - 1
    in_specs = [pl.BlockSpec((TB, D_MODEL), (lambda i: (x_last, 0)) if sample else (lambda i: (i, 0)))]
    args = [x]
    if sample:
        conv_in, state_in = sample_state
        in_specs += [pl.BlockSpec(conv_in.shape, const3), pl.BlockSpec(state_in.shape, const4)]
        args += [conv_in, state_in]
    in_specs += [pl.BlockSpec((D_MODEL, IN_COLS_PAD), const2)] * np_
    args += list(w["w_in"][:np_])
    in_specs += [pl.BlockSpec((LANES, GLA_QK), const2)] * np_
    args += list(w["wa2"][:np_])
    in_specs += [pl.BlockSpec((1, GLA_QK), const2), pl.BlockSpec((CONV_W, CONV_DIM), const2),
                 pl.BlockSpec((1, GLA_HEAD_V), const2)]
    args += [w["b_alpha"], w["conv_w"], w["gla_norm_g"]]
    in_specs += [pl.BlockSpec((D_MODEL, D_MODEL), const2)] * np_
    args += list(w["w_out"][:np_])
    in_specs += [pl.BlockSpec((1, D_MODEL), const2)] * 2
    args += [w["ln_g"], w["ln_b"]]
    if with_router:
        in_specs += [pl.BlockSpec((D_MODEL, LANES), const2)] * 2
        args += list(w["router"])
    aliases = {}
    n_alias = 0
    if sample:
        n_alias = len(into)
        for k, buf in enumerate(into):
            aliases[len(args)] = k
            in_specs.append(pl.BlockSpec(memory_space=pl.ANY))
            args.append(buf)

    out_row = (lambda i: (n_blocks, 0)) if sample else (lambda i: (i, 0))
    out_shape = [jax.ShapeDtypeStruct((n_rows, D_MODEL), F32)]
    out_specs = [pl.BlockSpec((TB, D_MODEL), out_row)]
    if with_router:
        out_shape += [jax.ShapeDtypeStruct((n_rows, D_MODEL), BF16),
                      jax.ShapeDtypeStruct((n_rows, LANES), F32)]
        out_specs += [pl.BlockSpec((TB, D_MODEL), out_row), pl.BlockSpec((TB, LANES), out_row)]
    if sample:
        out_shape += [jax.ShapeDtypeStruct(conv_in.shape, F32), jax.ShapeDtypeStruct(state_in.shape, F32)]
        out_specs += [pl.BlockSpec(conv_in.shape, const3), pl.BlockSpec(state_in.shape, const4)]
    else:
        n_seq = n_blocks // blocks_per_seq
        out_shape += [jax.ShapeDtypeStruct((n_seq, CONV_W - 1, CONV_DIM), F32),
                      jax.ShapeDtypeStruct((n_seq, GLA_HEADS, GLA_HEAD_K, GLA_HEAD_V), F32)]
        out_specs += [pl.BlockSpec((1, CONV_W - 1, CONV_DIM), lambda i: (i // blocks_per_seq, 0, 0)),
                      pl.BlockSpec((1, GLA_HEADS, GLA_HEAD_K, GLA_HEAD_V),
                                   lambda i: (i // blocks_per_seq, 0, 0, 0))]
    scratch = [
        pltpu.VMEM((TB, IN_COLS_PAD), F32),
        pltpu.VMEM((TB + 8, CONV_DIM), F32),
        pltpu.VMEM((8, CONV_DIM), F32),
        pltpu.VMEM((CHUNKS_PER_BLOCK, 16, CONV_DIM), F32),
        pltpu.VMEM((TB, D_MODEL), F32),
        pltpu.VMEM((TB, GLA_DV), F32),
    ]
    scratch += [pltpu.VMEM((TB, GLA_QK), BF16)] * np_
    scratch += [pltpu.VMEM((TB, GLA_QK), BF16)] * np_
    scratch += [pltpu.VMEM((TB, GLA_DV), BF16)] * np_
    scratch += [pltpu.VMEM((TB, GLA_QK), BF16)] * 2
    scratch += [pltpu.VMEM((GLA_HEADS, GLA_HEAD_K, GLA_HEAD_V), F32)]
    kern = functools.partial(_mixer_kernel, sample=sample, precise=precise, blocks_per_seq=blocks_per_seq,
                             with_router=with_router, n_alias=n_alias)
    name = ("mixer_sample" if sample else "mixer_prompt") + ("_router" if with_router else "")
    return pl.pallas_call(
        kern,
        grid=(1 if sample else n_blocks,),
        in_specs=in_specs,
        out_specs=out_specs,
        out_shape=out_shape,
        scratch_shapes=scratch,
        input_output_aliases=aliases,
        compiler_params=pltpu.CompilerParams(dimension_semantics=("arbitrary",),
                                             vmem_limit_bytes=VMEM_LIMIT_BYTES),
        name=name,
    )(*args)


def _swiglu_step(xp, wg, wu, wd, acc_ref, precise):
    f = pl.program_id(1)
    g = _mm(xp, _parts(wg, precise))
    u = _mm(xp, _parts(wu, precise))
    part = _mm(_parts(_silu(g) * u, precise), _parts(wd, precise))

    @pl.when(f == 0)
    def _():
        acc_ref[...] = part

    @pl.when(f > 0)
    def _():
        acc_ref[...] += part


def _swiglu_pipeline(load_x, wg_ref, wu_ref, wd_ref, widx, acc_ref, h_ref):
    f = pl.program_id(1)

    def front(slot):
        xb = load_x()
        g = _dot(xb, wg_ref[widx].astype(BF16))
        u = _dot(xb, wu_ref[widx].astype(BF16))
        h_ref[slot] = (_silu(g) * u).astype(BF16)

    def back(slot):
        acc_ref[...] += _dot(h_ref[slot], wd_ref[widx].astype(BF16))

    @pl.when(f == 0)
    def _():
        acc_ref[...] = jnp.zeros_like(acc_ref)
        front(0)

    for parity in (0, 1):
        @pl.when(jnp.logical_and(jnp.logical_and(f > 0, f < N_FF_SLICES), f % 2 == parity))
        def _():
            back(1 - parity)
            front(parity)

    @pl.when(f == N_FF_SLICES)
    def _():
        back((N_FF_SLICES - 1) % 2)


def _ffn_dense_kernel(*refs, n_alias):
    x_ref, wg_ref, wu_ref, wd_ref, lng_ref, lnb_ref = refs[:6]
    out_ref, acc_ref, h_ref = refs[6 + n_alias:]
    _swiglu_pipeline(lambda: x_ref[...].astype(BF16), wg_ref, wu_ref, wd_ref, 0, acc_ref, h_ref)

    @pl.when(pl.program_id(1) == N_FF_SLICES)
    def _():
        out_ref[...] = _layer_norm(ALPHA * x_ref[...] + acc_ref[...], lng_ref[...], lnb_ref[...])


def _ffn_dense_precise_kernel(*refs, n_alias):
    x_ref, wg_ref, wu_ref, wd_ref, lng_ref, lnb_ref = refs[:6]
    out_ref, acc_ref = refs[6 + n_alias:]
    x = x_ref[...]
    _swiglu_step(_parts(x, True), wg_ref[0], wu_ref[0], wd_ref[0], acc_ref, True)

    @pl.when(pl.program_id(1) == N_FF_SLICES - 1)
    def _():
        out_ref[...] = _layer_norm(ALPHA * x + acc_ref[...], lng_ref[...], lnb_ref[...])


def _ffn_dense_call(x, wg, wu, wd, e, ln_g, ln_b, *, tm, first_block, n_blocks, precise, into=None):
    n_rows = x.shape[0]
    if precise:
        n_steps = N_FF_SLICES
        front_slice = back_slice = lambda f: f
    else:
        n_steps = N_FF_SLICES + 1
        front_slice = lambda f: jnp.minimum(f, N_FF_SLICES - 1)
        back_slice = lambda f: jnp.maximum(f - 1, 0)
    row_spec = pl.BlockSpec((tm, D_MODEL), lambda t, f: (first_block + t, 0))
    in_specs = [row_spec,
                pl.BlockSpec((1, D_MODEL, TF), lambda t, f: (e, 0, front_slice(f))),
                pl.BlockSpec((1, D_MODEL, TF), lambda t, f: (e, 0, front_slice(f))),
                pl.BlockSpec((1, TF, D_MODEL), lambda t, f: (e, back_slice(f), 0)),
                pl.BlockSpec((1, D_MODEL), lambda t, f: (0, 0)),
                pl.BlockSpec((1, D_MODEL), lambda t, f: (0, 0))]
    args = [x, wg, wu, wd, ln_g, ln_b]
    aliases = {}
    n_alias = 0
    if into is not None:
        n_alias = 1
        aliases[len(args)] = 0
        in_specs.append(pl.BlockSpec(memory_space=pl.ANY))
        args.append(into)
    scratch = [pltpu.VMEM((tm, D_MODEL), F32)]
    if not precise:
        scratch.append(pltpu.VMEM((2, tm, TF), BF16))
    return pl.pallas_call(
        functools.partial(_ffn_dense_precise_kernel if precise else _ffn_dense_kernel, n_alias=n_alias),
        grid=(n_blocks, n_steps),
        in_specs=in_specs,
        out_specs=row_spec,
        out_shape=jax.ShapeDtypeStruct((n_rows, D_MODEL), F32),
        scratch_shapes=scratch,
        input_output_aliases=aliases,
        compiler_params=pltpu.CompilerParams(dimension_semantics=("arbitrary", "arbitrary"),
                                             vmem_limit_bytes=VMEM_LIMIT_BYTES),
        name="ffn_dense_precise" if precise else "ffn_dense",
    )(*args)


def _ffn_moe_kernel(te_ref, nt_ref, x_ref, wg_ref, wu_ref, wd_ref, *rest, first_tile, n_alias):
    out_ref, acc_ref, h_ref = rest[n_alias:]
    used = first_tile + pl.program_id(0) < nt_ref[0]

    @pl.when(used)
    def _():
        _swiglu_pipeline(lambda: x_ref[...], wg_ref, wu_ref, wd_ref, (0, 0), acc_ref, h_ref)

    last = pl.program_id(1) == N_FF_SLICES

    @pl.when(jnp.logical_and(last, used))
    def _():
        out_ref[...] = acc_ref[...]

    @pl.when(jnp.logical_and(last, jnp.logical_not(used)))
    def _():
        out_ref[...] = jnp.zeros_like(out_ref)


def _ffn_moe_call(tile_expert, n_tiles_used, xs, wg, wu, wd, e, *, first_tile, n_sorted, into=None):
    front_slice = lambda f: jnp.minimum(f, N_FF_SLICES - 1)
    back_slice = lambda f: jnp.maximum(f - 1, 0)
    in_specs = [
        pl.BlockSpec((TM_MOE, D_MODEL), lambda t, f, te, nt: (t, 0)),
        pl.BlockSpec((1, 1, D_MODEL, TF), lambda t, f, te, nt: (e, te[first_tile + t], 0, front_slice(f))),
        pl.BlockSpec((1, 1, D_MODEL, TF), lambda t, f, te, nt: (e, te[first_tile + t], 0, front_slice(f))),
        pl.BlockSpec((1, 1, TF, D_MODEL), lambda t, f, te, nt: (e, te[first_tile + t], back_slice(f), 0)),
    ]
    args = [tile_expert, n_tiles_used, xs, wg, wu, wd]
    aliases = {}
    n_alias = 0
    if into is not None:
        n_alias = 1
        aliases[len(args)] = 0
        in_specs.append(pl.BlockSpec(memory_space=pl.ANY))
        args.append(into)
    grid_spec = pltpu.PrefetchScalarGridSpec(
        num_scalar_prefetch=2,
        grid=(xs.shape[0] // TM_MOE, N_FF_SLICES + 1),
        in_specs=in_specs,
        out_specs=pl.BlockSpec((TM_MOE, D_MODEL), lambda t, f, te, nt: (first_tile + t, 0)),
        scratch_shapes=[pltpu.VMEM((TM_MOE, D_MODEL), F32), pltpu.VMEM((2, TM_MOE, TF), BF16)],
    )
    return pl.pallas_call(
        functools.partial(_ffn_moe_kernel, first_tile=first_tile, n_alias=n_alias),
        grid_spec=grid_spec,
        out_shape=jax.ShapeDtypeStruct((n_sorted, D_MODEL), F32),
        input_output_aliases=aliases,
        compiler_params=pltpu.CompilerParams(dimension_semantics=("arbitrary", "arbitrary"),
                                             vmem_limit_bytes=VMEM_LIMIT_BYTES),
        name="ffn_moe",
    )(*args)


def _combine_kernel(x_ref, ya_ref, yb_ref, route_ref, lng_ref, lnb_ref, out_ref):
    slab = route_ref[...]
    lane = lax.broadcasted_iota(jnp.int32, slab.shape, 1)
    g1 = jnp.sum(jnp.where(lane == LANE_G1, slab, 0.0), axis=-1, keepdims=True)
    g2 = jnp.sum(jnp.where(lane == LANE_G2, slab, 0.0), axis=-1, keepdims=True)
    r = ALPHA * x_ref[...] + (g1 * ya_ref[...] + g2 * yb_ref[...])
    out_ref[...] = _layer_norm(r, lng_ref[...], lnb_ref[...])


def _combine_call(x, ya, yb, route, ln_g, ln_b, *, first_block, n_blocks):
    in_spec = pl.BlockSpec((TB, D_MODEL), lambda t: (first_block + t, 0))
    route_spec = pl.BlockSpec((TB, LANES), lambda t: (first_block + t, 0))
    vec_spec = pl.BlockSpec((1, D_MODEL), lambda t: (0, 0))
    return pl.pallas_call(
        _combine_kernel,
        grid=(n_blocks,),
        in_specs=[in_spec, in_spec, in_spec, route_spec, vec_spec, vec_spec],
        out_specs=pl.BlockSpec((TB, D_MODEL), lambda t: (t, 0)),
        out_shape=jax.ShapeDtypeStruct((n_blocks * TB, D_MODEL), F32),
        compiler_params=pltpu.CompilerParams(dimension_semantics=("arbitrary",),
                                             vmem_limit_bytes=VMEM_LIMIT_BYTES),
        name="moe_combine",
    )(x, ya, yb, route, ln_g, ln_b)


def _take_rows(a, idx):
    return a.at[idx].get(mode="promise_in_bounds")


def _moe_dispatch(route, n_rows):
    i1 = route[:, LANE_I1].astype(jnp.int32)
    i2 = route[:, LANE_I2].astype(jnp.int32)
    experts = jnp.arange(N_EXPERTS, dtype=jnp.int32)
    hit = ((i1[:, None] == experts) | (i2[:, None] == experts)).astype(jnp.int32)
    before = jnp.cumsum(hit, axis=0) - hit
    counts = jnp.sum(hit, axis=0)
    padded = ((counts + TM_MOE - 1) // TM_MOE) * TM_MOE
    ends = jnp.cumsum(padded)
    starts = ends - padded
    rank1 = jnp.sum(jnp.where(i1[:, None] == experts, before, 0), axis=1)
    rank2 = jnp.sum(jnp.where(i2[:, None] == experts, before, 0), axis=1)
    p1 = _take_rows(starts, i1) + rank1
    p2 = _take_rows(starts, i2) + rank2
    n_tiles = (2 * n_rows + N_EXPERTS * (TM_MOE - 1) + TM_MOE - 1) // TM_MOE
    tok = jnp.arange(n_rows, dtype=jnp.int32)
    row_src = jnp.zeros((n_tiles * TM_MOE,), jnp.int32).at[jnp.concatenate([p1, p2])].set(
        jnp.concatenate([tok, tok]), mode="promise_in_bounds", unique_indices=True)
    tile_start = jnp.arange(n_tiles, dtype=jnp.int32) * TM_MOE
    tile_expert = jnp.minimum(jnp.sum(tile_start[:, None] >= ends[None, :], axis=1), N_EXPERTS - 1)
    n_tiles_used = (ends[-1] // TM_MOE).reshape(1)
    return row_src, p1, p2, tile_expert.astype(jnp.int32), n_tiles_used.astype(jnp.int32)


def _split_kernel(a_ref, hi_ref, lo_ref):
    hi, lo = _split_bf16(a_ref[...])
    hi_ref[...] = hi
    lo_ref[...] = lo


def _split_call(a):
    rows, cols = a.shape
    tr = min(rows, 256)
    spec = pl.BlockSpec((tr, cols), lambda r: (r, 0))
    return pl.pallas_call(
        _split_kernel,
        grid=(rows // tr,),
        in_specs=[spec],
        out_specs=[spec, spec],
        out_shape=[jax.ShapeDtypeStruct(a.shape, BF16)] * 2,
        name="split_bf16",
    )(a)


def _pad_w_in(w_in):
    main = w_in[:, :COL_A]
    alr = jnp.pad(w_in[:, COL_A:], ((0, 0), (0, LANES - GATE_RANK)))
    return jnp.concatenate([main, alr], axis=1)


def kernel(x_prompt, x_sample, cache_conv, state_gla, ln_mix_g, ln_mix_b, w_in, w_alpha2, b_alpha,
           conv_w, gla_norm_g, w_out, ln_ffn_g, ln_ffn_b, ffn_w_gate, ffn_w_up, ffn_w_down,
           moe_router, moe_w_gate, moe_w_up, moe_w_down):
    n_prompt_seq, seq_len, _ = x_prompt.shape
    n_sample, dec_len, _ = x_sample.shape
    assert dec_len == CHUNK and n_sample * dec_len == TB and seq_len % TB == 0
    n_prompt_rows = n_prompt_seq * seq_len
    n_rows = n_prompt_rows + TB
    blocks_per_seq = seq_len // TB
    n_prompt_blocks = n_prompt_rows // TB
    assert n_prompt_rows % TM_DENSE == 0

    xp = x_prompt.reshape(n_prompt_rows, D_MODEL)
    xs = x_sample.reshape(TB, D_MODEL)
    gla_p, conv_p, gla_s, conv_s = [], [], [], []
    for l in range(DEPTH):
        moe_layer = l % 2 == 1
        e = l // 2
        w = {
            "w_in": _split_call(_pad_w_in(w_in[l])),
            "wa2": _split_call(jnp.pad(w_alpha2[l], ((0, LANES - GATE_RANK), (0, 0)))),
            "b_alpha": b_alpha[l][None, :],
            "conv_w": conv_w[l],
            "gla_norm_g": gla_norm_g[l][None, :],
            "w_out": _split_call(w_out[l]),
            "ln_g": ln_mix_g[l][None, :],
            "ln_b": ln_mix_b[l][None, :],
        }
        if moe_layer:
            w["router"] = _split_call(jnp.pad(moe_router[e], ((0, 0), (0, LANES - N_EXPERTS))))
        common = dict(n_rows=n_rows, n_blocks=n_prompt_blocks, blocks_per_seq=blocks_per_seq,
                      with_router=moe_layer)
        outs = _mixer_call(xp, w, **common)
        n_stream = 3 if moe_layer else 1
        cp, sp = outs[n_stream:]
        outs_s = _mixer_call(xs, w, sample_state=(cache_conv[l], state_gla[l]), into=outs[:n_stream], **common)
        cs, ss = outs_s[n_stream:]
        conv_p.append(cp)
        gla_p.append(sp)
        conv_s.append(cs)
        gla_s.append(ss)

        ln_g = ln_ffn_g[l][None, :]
        ln_b = ln_ffn_b[l][None, :]
        if moe_layer:
            x1, x1b, route = outs_s[:3]
            row_src, p1, p2, tile_expert, n_tiles_used = _moe_dispatch(route, n_rows)
            n_tiles = row_src.shape[0] // TM_MOE
            bounds = [n_tiles * c // MOE_CALLS for c in range(MOE_CALLS + 1)]
            ys = None
            for t0, t1 in zip(bounds[:-1], bounds[1:]):
                xsort = _take_rows(x1b, row_src[t0 * TM_MOE:t1 * TM_MOE])
                ys = _ffn_moe_call(tile_expert, n_tiles_used, xsort, moe_w_gate, moe_w_up, moe_w_down, e,
                                   first_tile=t0, n_sorted=row_src.shape[0], into=ys)
            comb = functools.partial(_combine_call, x1, _take_rows(ys, p1), _take_rows(ys, p2), route, ln_g, ln_b)
            if l == DEPTH - 1:
                xp = comb(first_block=0, n_blocks=n_prompt_blocks)
                xs = comb(first_block=n_prompt_blocks, n_blocks=1)
            else:
                xp = xs = comb(first_block=0, n_blocks=n_prompt_blocks + 1)
        else:
            x1 = outs_s[0]
            dense = functools.partial(_ffn_dense_call, x1, ffn_w_gate, ffn_w_up, ffn_w_down, e, ln_g, ln_b)
            x2 = dense(tm=TM_DENSE, first_block=0, n_blocks=n_prompt_rows // TM_DENSE, precise=False)
            xp = xs = dense(tm=TB, first_block=n_prompt_blocks, n_blocks=1, precise=(l == 0), into=x2)

    y_prompt = xp[:n_prompt_rows].reshape(n_prompt_seq, seq_len, D_MODEL)
    y_sample = xs[xs.shape[0] - TB:].reshape(n_sample, dec_len, D_MODEL)
    return (y_prompt, y_sample, jnp.stack(gla_p), jnp.stack(conv_p), jnp.stack(gla_s), jnp.stack(conv_s))
```

```python
import functools

import jax
import jax.numpy as jnp
from jax import lax
from jax.experimental import pallas as pl
from jax.experimental.pallas import tpu as pltpu

F32 = jnp.float32
BF16 = jnp.bfloat16

D_MODEL = 1024
DEPTH = 4
CHUNK = 64
CONV_DIM = 512
CONV_W = 3
GLA_HEADS = 4
GLA_HEAD_K = 64
GLA_HEAD_V = 128
GLA_QK = GLA_HEADS * GLA_HEAD_K
GLA_DV = GLA_HEADS * GLA_HEAD_V
GATE_RANK = 16
GATE_TEMP = 16.0
D_FF = 3584
N_EXPERTS = 8
ALPHA = (2 * DEPTH) ** 0.25
LN_EPS = 1e-5

LANES = 128
SUBLANES = 8
VMEM_LIMIT_BYTES = 56 * 1024 * 1024

COL_CB, COL_CC, COL_CH = 0, CONV_DIM, 2 * CONV_DIM
COL_Q = 3 * CONV_DIM
COL_K = COL_Q + GLA_QK
COL_V = COL_K + GLA_QK
COL_G = COL_V + GLA_DV
COL_A = COL_G + GLA_DV
IN_COLS_PAD = COL_A + LANES

TB = 512
SUB = 256
CHUNKS_PER_BLOCK = TB // CHUNK
TF = 512
N_FF_SLICES = D_FF // TF
TM_MOE = 1024
TM_DENSE = 1024
MOE_CALLS = 4
LANE_I1, LANE_I2, LANE_G1, LANE_G2 = 8, 9, 10, 11


def _dot(a, b):
    return jnp.dot(a, b, preferred_element_type=F32)


def _dot_tn(a, b):
    return lax.dot_general(a, b, (((0,), (0,)), ((), ())), preferred_element_type=F32)


def _dot_nt(a, b):
    return lax.dot_general(a, b, (((1,), (1,)), ((), ())), preferred_element_type=F32)


def _split_bf16(a):
    hi = a.astype(BF16)
    lo = (a - hi.astype(F32)).astype(BF16)
    return hi, lo


def _parts(a, precise):
    return _split_bf16(a) if precise else (a.astype(BF16),)


def _mm(a, b, dot=_dot):
    r = dot(a[0], b[0])
    if len(a) > 1:
        r = r + dot(a[1], b[0])
    if len(b) > 1:
        r = r + dot(a[0], b[1])
    return r


def _load(refs, idx=None):
    return tuple(r[...] if idx is None else r[idx] for r in refs)


def _store(refs, idx, parts):
    for r, p in zip(refs, parts):
        r[idx] = p


def _layer_norm(r, g, b):
    mu = jnp.mean(r, axis=-1, keepdims=True)
    d = r - mu
    var = jnp.mean(d * d, axis=-1, keepdims=True)
    return d * lax.rsqrt(var + LN_EPS) * g + b


def _silu(g):
    return g / (1.0 + jnp.exp(-g))


def _mixer_kernel(*refs, sample, precise, blocks_per_seq, with_router, n_alias):
    np_ = 2 if precise else 1
    it = iter(refs)

    def take(n):
        return [next(it) for _ in range(n)]

    x_ref = next(it)
    if not sample:
        xn_ref = next(it)
    if sample:
        convin_ref = next(it)
        statein_ref = next(it)
    w_in_refs = take(np_)
    wa2_refs = take(np_)
    balpha_ref = next(it)
    convw_ref = next(it)
    gnorm_ref = next(it)
    w_out_refs = take(np_)
    lng_ref = next(it)
    lnb_ref = next(it)
    if with_router:
        rhi_ref = next(it)
        rlo_ref = next(it)
    take(n_alias)
    x1_ref = next(it)
    if with_router:
        x1b_ref = next(it)
        route_ref = next(it)
    convout_ref = next(it)
    stateout_ref = next(it)
    proj_ref = next(it)
    ue_ref = next(it)
    prev_ref = next(it)
    fx_ref = next(it)
    y_ref = next(it)
    o_ref = next(it)
    qg_refs = take(np_)
    kd_refs = take(np_)
    v_refs = take(np_)
    la_refs = take(2)
    s_ref = next(it)
    qms_refs = take(np_)
    if not sample:
        xnb_ref = next(it)

    i = pl.program_id(0)
    seq_start = (i % blocks_per_seq) == 0

    def in_proj(src, cols):
        proj_ref[:, cols] = _mm(src, tuple(r[:, cols] for r in w_in_refs))

    if sample:
        in_proj(_parts(x_ref[...], precise), slice(None))
        project_ahead = lambda cols: None
    else:
        @pl.when(i == 0)
        def _():
            in_proj(_parts(x_ref[...], precise), slice(None))

        xnb_ref[...] = xn_ref[...].astype(BF16)
        project_ahead = lambda cols: in_proj((xnb_ref[...],), cols)

    if sample:
        prev_ref[6:8, :] = convin_ref[0]
    else:
        @pl.when(seq_start)
        def _():
            prev_ref[...] = jnp.zeros_like(prev_ref)

    w0 = convw_ref[0:1, :]
    w1 = convw_ref[1:2, :]
    w2 = convw_ref[2:3, :]
    u = proj_ref[:, COL_CC:COL_CC + CONV_DIM] * proj_ref[:, COL_CH:COL_CH + CONV_DIM]
    ue_ref[0:8, :] = prev_ref[...]
    ue_ref[8:8 + TB, :] = u
    conv = ue_ref[6:6 + TB, :] * w0 + ue_ref[7:7 + TB, :] * w1 + u * w2
    y_ref[:, 0:CONV_DIM] = proj_ref[:, COL_CB:COL_CB + CONV_DIM] * conv
    if sample:
        for c in range(CHUNKS_PER_BLOCK):
            r0 = c * CHUNK
            if c > 0:
                fx_ref[c, 6:8, :] = convin_ref[c]
                fx_ref[c, 8:16, :] = ue_ref[8 + r0:16 + r0, :]
                convc = fx_ref[c, 6:14, :] * w0 + fx_ref[c, 7:15, :] * w1 + fx_ref[c, 8:16, :] * w2
                y_ref[r0:r0 + 8, 0:CONV_DIM] = proj_ref[r0:r0 + 8, COL_CB:COL_CB + CONV_DIM] * convc
            convout_ref[c] = ue_ref[8 + r0 + CHUNK - 2:8 + r0 + CHUNK, :]
    else:
        prev_ref[...] = ue_ref[TB:TB + 8, :]
        convout_ref[0] = ue_ref[TB + 6:TB + 8, :]
    project_ahead(slice(COL_CB, COL_Q))

    z = _mm(_parts(proj_ref[:, COL_A:COL_A + LANES], precise), _load(wa2_refs)) + balpha_ref[...]
    la = (jnp.minimum(z, 0.0) - jnp.log(1.0 + jnp.exp(-jnp.abs(z)))) * (1.0 / GATE_TEMP)
    _store(la_refs, slice(None), _split_bf16(la))

    row = lax.broadcasted_iota(jnp.int32, (SUB, SUB), 0)
    col = lax.broadcasted_iota(jnp.int32, (SUB, SUB), 1)
    same_chunk = (row // CHUNK) == (col // CHUNK)
    causal = jnp.logical_and(same_chunk, row >= col)
    tril_bd = (causal.astype(BF16),)
    ones_bd = (same_chunk.astype(BF16),)
    lane_head = lax.broadcasted_iota(jnp.int32, (SUB, GLA_QK), 1) // GLA_HEAD_K

    for sb in range(TB // SUB):
        rs = slice(sb * SUB, (sb + 1) * SUB)
        lap = _load(la_refs, (rs, slice(None)))
        bcum = _mm(tril_bd, lap)
        blast = _mm(ones_bd, lap)
        q = proj_ref[rs, COL_Q:COL_Q + GLA_QK]
        k = proj_ref[rs, COL_K:COL_K + GLA_QK]
        qg = q * jnp.exp(bcum) * (GLA_HEAD_K ** -0.5)
        kg = _parts(k * jnp.exp(-bcum), precise)
        _store(qg_refs, (rs, slice(None)), _parts(qg, precise))
        _store(kd_refs, (rs, slice(None)), _parts(k * jnp.exp(blast - bcum), precise))
        vp = _parts(proj_ref[rs, COL_V:COL_V + GLA_DV], precise)
        _store(v_refs, (rs, slice(None)), vp)
        for h in range(GLA_HEADS):
            vs = slice(h * GLA_HEAD_V, (h + 1) * GLA_HEAD_V)
            qm = _parts(jnp.where(lane_head == h, qg, 0.0), precise)
            for cc in range(SUB // CHUNK):
                dst = (sb * (SUB // CHUNK) + cc, slice(h * CHUNK, (h + 1) * CHUNK), slice(None))
                _store(qms_refs, dst, tuple(p[cc * CHUNK:(cc + 1) * CHUNK] for p in qm))
            sc = jnp.where(causal, _mm(qm, kg, _dot_nt), 0.0)
            o_ref[rs, vs] = _mm(_parts(sc, precise), tuple(p[:, vs] for p in vp))

    project_ahead(slice(COL_Q, COL_G))
    project_ahead(slice(COL_A, IN_COLS_PAD))
    trow = lax.broadcasted_iota(jnp.int32, (TB, LANES), 0) // CHUNK
    tcol = lax.broadcasted_iota(jnp.int32, (TB, LANES), 1)
    chunk_sel = ((trow == tcol).astype(BF16),)
    dec_all = jnp.exp(_mm(_load(la_refs), chunk_sel, _dot_tn))
    if not sample:
        s = jnp.where(seq_start, 0.0, s_ref[...])
    for c in range(CHUNKS_PER_BLOCK):
        cs = slice(c * CHUNK, (c + 1) * CHUNK)
        if sample:
            s = statein_ref[c].reshape(GLA_QK, GLA_HEAD_V)
        oi = _mm(_load(qms_refs, c), _parts(s, precise))
        for h in range(GLA_HEADS):
            o_ref[cs, h * GLA_HEAD_V:(h + 1) * GLA_HEAD_V] += oi[h * CHUNK:(h + 1) * CHUNK, :]
        ds = _mm(_load(kd_refs, (cs, slice(None))), _load(v_refs, (cs, slice(None))), _dot_tn)
        ds_diag = jnp.concatenate([ds[h * GLA_HEAD_K:(h + 1) * GLA_HEAD_K, h * GLA_HEAD_V:(h + 1) * GLA_HEAD_V]
                                   for h in range(GLA_HEADS)], axis=0)
        s = dec_all[:, c:c + 1] * s + ds_diag
        if sample:
            stateout_ref[c] = s.reshape(GLA_HEADS, GLA_HEAD_K, GLA_HEAD_V)
    if not sample:
        s_ref[...] = s
        stateout_ref[0] = s.reshape(GLA_HEADS, GLA_HEAD_K, GLA_HEAD_V)

    gn = gnorm_ref[...]
    for h in range(GLA_HEADS):
        vs = slice(h * GLA_HEAD_V, (h + 1) * GLA_HEAD_V)
        oh = o_ref[:, vs]
        ms = jnp.mean(oh * oh, axis=-1, keepdims=True)
        g = proj_ref[:, COL_G + h * GLA_HEAD_V:COL_G + (h + 1) * GLA_HEAD_V]
        y_ref[:, CONV_DIM + h * GLA_HEAD_V:CONV_DIM + (h + 1) * GLA_HEAD_V] = (
            oh * lax.rsqrt(ms + LN_EPS) * gn * _silu(g))

    project_ahead(slice(COL_G, COL_A))

    m = _mm(_parts(y_ref[...], precise), _load(w_out_refs))
    x1 = _layer_norm(ALPHA * x_ref[...] + m, lng_ref[...], lnb_ref[...])
    x1_ref[...] = x1

    if with_router:
        x1b_ref[...] = x1.astype(BF16)
        lg = _mm(_split_bf16(x1), (rhi_ref[...], rlo_ref[...]))
        lane = lax.broadcasted_iota(jnp.int32, lg.shape, 1)
        neg = jnp.float32(-jnp.inf)
        lg = jnp.where(lane < N_EXPERTS, lg, neg)
        m1 = jnp.max(lg, axis=-1, keepdims=True)
        i1 = jnp.min(jnp.where(lg == m1, lane, LANES), axis=-1, keepdims=True)
        lg2 = jnp.where(lane == i1, neg, lg)
        m2 = jnp.max(lg2, axis=-1, keepdims=True)
        i2 = jnp.min(jnp.where(lg2 == m2, lane, LANES), axis=-1, keepdims=True)
        e2 = jnp.exp(m2 - m1)
        g1 = 1.0 / (1.0 + e2)
        g2 = e2 / (1.0 + e2)
        slab = jnp.where(lane == i1, g1, jnp.where(lane == i2, g2, 0.0))
        slab = jnp.where(lane == LANE_I1, i1.astype(F32), slab)
        slab = jnp.where(lane == LANE_I2, i2.astype(F32), slab)
        slab = jnp.where(lane == LANE_G1, g1, slab)
        slab = jnp.where(lane == LANE_G2, g2, slab)
        route_ref[...] = slab


def _mixer_call(x, w, *, n_rows, n_blocks, blocks_per_seq, with_router, sample_state=None, into=None):
    sample = sample_state is not None
    precise = sample
    np_ = 2 if precise else 1
    const2 = lambda i: (0, 0)
    const3 = lambda i: (0, 0, 0)
    const4 = lambda i: (0, 0, 0, 0)

    x_last = x.shape[0] // TB - 1
    in_specs = [pl.BlockSpec((TB, D_MODEL), (lambda i: (x_last, 0)) if sample else (lambda i: (i, 0)))]
    args = [x]
    if not sample:
        in_specs.append(pl.BlockSpec((TB, D_MODEL), lambda i: (jnp.minimum(i + 1, n_blocks - 1), 0)))
        args.append(x)
    if sample:
        conv_in, state_in = sample_state
        in_specs += [pl.BlockSpec(conv_in.shape, const3), pl.BlockSpec(state_in.shape, const4)]
        args += [conv_in, state_in]
    in_specs += [pl.BlockSpec((D_MODEL, IN_COLS_PAD), const2)] * np_
    args += list(w["w_in"][:np_])
    in_specs += [pl.BlockSpec((LANES, GLA_QK), const2)] * np_
    args += list(w["wa2"][:np_])
    in_specs += [pl.BlockSpec((1, GLA_QK), const2), pl.BlockSpec((CONV_W, CONV_DIM), const2),
                 pl.BlockSpec((1, GLA_HEAD_V), const2)]
    args += [w["b_alpha"], w["conv_w"], w["gla_norm_g"]]
    in_specs += [pl.BlockSpec((D_MODEL, D_MODEL), const2)] * np_
    args += list(w["w_out"][:np_])
    in_specs += [pl.BlockSpec((1, D_MODEL), const2)] * 2
    args += [w["ln_g"], w["ln_b"]]
    if with_router:
        in_specs += [pl.BlockSpec((D_MODEL, LANES), const2)] * 2
        args += list(w["router"])
    aliases = {}
    n_alias = 0
    if sample:
        n_alias = len(into)
        for k, buf in enumerate(into):
            aliases[len(args)] = k
            in_specs.append(pl.BlockSpec(memory_space=pl.ANY))
            args.append(buf)

    out_row = (lambda i: (n_blocks, 0)) if sample else (lambda i: (i, 0))
    out_shape = [jax.ShapeDtypeStruct((n_rows, D_MODEL), F32)]
    out_specs = [pl.BlockSpec((TB, D_MODEL), out_row)]
    if with_router:
        out_shape += [jax.ShapeDtypeStruct((n_rows, D_MODEL), BF16),
                      jax.ShapeDtypeStruct((n_rows, LANES), F32)]
        out_specs += [pl.BlockSpec((TB, D_MODEL), out_row), pl.BlockSpec((TB, LANES), out_row)]
    if sample:
        out_shape += [jax.ShapeDtypeStruct(conv_in.shape, F32), jax.ShapeDtypeStruct(state_in.shape, F32)]
        out_specs += [pl.BlockSpec(conv_in.shape, const3), pl.BlockSpec(state_in.shape, const4)]
    else:
        n_seq = n_blocks // blocks_per_seq
        out_shape += [jax.ShapeDtypeStruct((n_seq, CONV_W - 1, CONV_DIM), F32),
                      jax.ShapeDtypeStruct((n_seq, GLA_HEADS, GLA_HEAD_K, GLA_HEAD_V), F32)]
        out_specs += [pl.BlockSpec((1, CONV_W - 1, CONV_DIM), lambda i: (i // blocks_per_seq, 0, 0)),
                      pl.BlockSpec((1, GLA_HEADS, GLA_HEAD_K, GLA_HEAD_V),
                                   lambda i: (i // blocks_per_seq, 0, 0, 0))]
    scratch = [
        pltpu.VMEM((TB, IN_COLS_PAD), F32),
        pltpu.VMEM((TB + 8, CONV_DIM), F32),
        pltpu.VMEM((8, CONV_DIM), F32),
        pltpu.VMEM((CHUNKS_PER_BLOCK, 16, CONV_DIM), F32),
        pltpu.VMEM((TB, D_MODEL), F32),
        pltpu.VMEM((TB, GLA_DV), F32),
    ]
    scratch += [pltpu.VMEM((TB, GLA_QK), BF16)] * np_
    scratch += [pltpu.VMEM((TB, GLA_QK), BF16)] * np_
    scratch += [pltpu.VMEM((TB, GLA_DV), BF16)] * np_
    scratch += [pltpu.VMEM((TB, GLA_QK), BF16)] * 2
    scratch += [pltpu.VMEM((GLA_QK, GLA_HEAD_V), F32)]
    scratch += [pltpu.VMEM((CHUNKS_PER_BLOCK, GLA_HEADS * CHUNK, GLA_QK), BF16)] * np_
    if not sample:
        scratch += [pltpu.VMEM((TB, D_MODEL), BF16)]
    kern = functools.partial(_mixer_kernel, sample=sample, precise=precise, blocks_per_seq=blocks_per_seq,
                             with_router=with_router, n_alias=n_alias)
    name = ("mixer_sample" if sample else "mixer_prompt") + ("_router" if with_router else "")
    return pl.pallas_call(
        kern,
        grid=(1 if sample else n_blocks,),
        in_specs=in_specs,
        out_specs=out_specs,
        out_shape=out_shape,
        scratch_shapes=scratch,
        input_output_aliases=aliases,
        compiler_params=pltpu.CompilerParams(dimension_semantics=("arbitrary",),
                                             vmem_limit_bytes=VMEM_LIMIT_BYTES),
        name=name,
    )(*args)


def _swiglu_step(xp, wg, wu, wd, acc_ref, precise):
    f = pl.program_id(1)
    g = _mm(xp, _parts(wg, precise))
    u = _mm(xp, _parts(wu, precise))
    part = _mm(_parts(_silu(g) * u, precise), _parts(wd, precise))

    @pl.when(f == 0)
    def _():
        acc_ref[...] = part

    @pl.when(f > 0)
    def _():
        acc_ref[...] += part


def _swiglu_pipeline(load_x, wg_ref, wu_ref, wd_ref, widx, acc_ref, h_ref):
    f = pl.program_id(1)

    def front(slot):
        xb = load_x()
        g = _dot(xb, wg_ref[widx].astype(BF16))
        u = _dot(xb, wu_ref[widx].astype(BF16))
        h_ref[slot] = (_silu(g) * u).astype(BF16)

    def back(slot):
        acc_ref[...] += _dot(h_ref[slot], wd_ref[widx].astype(BF16))

    @pl.when(f == 0)
    def _():
        acc_ref[...] = jnp.zeros_like(acc_ref)
        front(0)

    for parity in (0, 1):
        @pl.when(jnp.logical_and(jnp.logical_and(f > 0, f < N_FF_SLICES), f % 2 == parity))
        def _():
            back(1 - parity)
            front(parity)

    @pl.when(f == N_FF_SLICES)
    def _():
        back((N_FF_SLICES - 1) % 2)


def _ffn_dense_kernel(*refs, n_alias):
    x_ref, wg_ref, wu_ref, wd_ref, lng_ref, lnb_ref = refs[:6]
    out_ref, acc_ref, h_ref = refs[6 + n_alias:]
    _swiglu_pipeline(lambda: x_ref[...].astype(BF16), wg_ref, wu_ref, wd_ref, 0, acc_ref, h_ref)

    @pl.when(pl.program_id(1) == N_FF_SLICES)
    def _():
        out_ref[...] = _layer_norm(ALPHA * x_ref[...] + acc_ref[...], lng_ref[...], lnb_ref[...])


def _ffn_dense_precise_kernel(*refs, n_alias):
    x_ref, wg_ref, wu_ref, wd_ref, lng_ref, lnb_ref = refs[:6]
    out_ref, acc_ref = refs[6 + n_alias:]
    x = x_ref[...]
    _swiglu_step(_parts(x, True), wg_ref[0], wu_ref[0], wd_ref[0], acc_ref, True)

    @pl.when(pl.program_id(1) == N_FF_SLICES - 1)
    def _():
        out_ref[...] = _layer_norm(ALPHA * x + acc_ref[...], lng_ref[...], lnb_ref[...])


def _ffn_dense_call(x, wg, wu, wd, e, ln_g, ln_b, *, tm, first_block, n_blocks, precise, into=None):
    n_rows = x.shape[0]
    if precise:
        n_steps = N_FF_SLICES
        front_slice = back_slice = lambda f: f
    else:
        n_steps = N_FF_SLICES + 1
        front_slice = lambda f: jnp.minimum(f, N_FF_SLICES - 1)
        back_slice = lambda f: jnp.maximum(f - 1, 0)
    row_spec = pl.BlockSpec((tm, D_MODEL), lambda t, f: (first_block + t, 0))
    in_specs = [row_spec,
                pl.BlockSpec((1, D_MODEL, TF), lambda t, f: (e, 0, front_slice(f))),
                pl.BlockSpec((1, D_MODEL, TF), lambda t, f: (e, 0, front_slice(f))),
                pl.BlockSpec((1, TF, D_MODEL), lambda t, f: (e, back_slice(f), 0)),
                pl.BlockSpec((1, D_MODEL), lambda t, f: (0, 0)),
                pl.BlockSpec((1, D_MODEL), lambda t, f: (0, 0))]
    args = [x, wg, wu, wd, ln_g, ln_b]
    aliases = {}
    n_alias = 0
    if into is not None:
        n_alias = 1
        aliases[len(args)] = 0
        in_specs.append(pl.BlockSpec(memory_space=pl.ANY))
        args.append(into)
    scratch = [pltpu.VMEM((tm, D_MODEL), F32)]
    if not precise:
        scratch.append(pltpu.VMEM((2, tm, TF), BF16))
    return pl.pallas_call(
        functools.partial(_ffn_dense_precise_kernel if precise else _ffn_dense_kernel, n_alias=n_alias),
        grid=(n_blocks, n_steps),
        in_specs=in_specs,
        out_specs=row_spec,
        out_shape=jax.ShapeDtypeStruct((n_rows, D_MODEL), F32),
        scratch_shapes=scratch,
        input_output_aliases=aliases,
        compiler_params=pltpu.CompilerParams(dimension_semantics=("arbitrary", "arbitrary"),
                                             vmem_limit_bytes=VMEM_LIMIT_BYTES),
        name="ffn_dense_precise" if precise else "ffn_dense",
    )(*args)


def _ffn_moe_kernel(te_ref, nt_ref, x_ref, wg_ref, wu_ref, wd_ref, *rest, first_tile, n_alias):
    out_ref, acc_ref, h_ref = rest[n_alias:]
    used = first_tile + pl.program_id(0) < nt_ref[0]

    @pl.when(used)
    def _():
        _swiglu_pipeline(lambda: x_ref[...], wg_ref, wu_ref, wd_ref, (0, 0), acc_ref, h_ref)

    last = pl.program_id(1) == N_FF_SLICES

    @pl.when(jnp.logical_and(last, used))
    def _():
        out_ref[...] = acc_ref[...]

    @pl.when(jnp.logical_and(last, jnp.logical_not(used)))
    def _():
        out_ref[...] = jnp.zeros_like(out_ref)


def _ffn_moe_call(tile_expert, n_tiles_used, xs, wg, wu, wd, e, *, first_tile, n_sorted, into=None):
    front_slice = lambda f: jnp.minimum(f, N_FF_SLICES - 1)
    back_slice = lambda f: jnp.maximum(f - 1, 0)
    in_specs = [
        pl.BlockSpec((TM_MOE, D_MODEL), lambda t, f, te, nt: (t, 0)),
        pl.BlockSpec((1, 1, D_MODEL, TF), lambda t, f, te, nt: (e, te[first_tile + t], 0, front_slice(f))),
        pl.BlockSpec((1, 1, D_MODEL, TF), lambda t, f, te, nt: (e, te[first_tile + t], 0, front_slice(f))),
        pl.BlockSpec((1, 1, TF, D_MODEL), lambda t, f, te, nt: (e, te[first_tile + t], back_slice(f), 0)),
    ]
    args = [tile_expert, n_tiles_used, xs, wg, wu, wd]
    aliases = {}
    n_alias = 0
    if into is not None:
        n_alias = 1
        aliases[len(args)] = 0
        in_specs.append(pl.BlockSpec(memory_space=pl.ANY))
        args.append(into)
    grid_spec = pltpu.PrefetchScalarGridSpec(
        num_scalar_prefetch=2,
        grid=(xs.shape[0] // TM_MOE, N_FF_SLICES + 1),
        in_specs=in_specs,
        out_specs=pl.BlockSpec((TM_MOE, D_MODEL), lambda t, f, te, nt: (first_tile + t, 0)),
        scratch_shapes=[pltpu.VMEM((TM_MOE, D_MODEL), F32), pltpu.VMEM((2, TM_MOE, TF), BF16)],
    )
    return pl.pallas_call(
        functools.partial(_ffn_moe_kernel, first_tile=first_tile, n_alias=n_alias),
        grid_spec=grid_spec,
        out_shape=jax.ShapeDtypeStruct((n_sorted, D_MODEL), F32),
        input_output_aliases=aliases,
        compiler_params=pltpu.CompilerParams(dimension_semantics=("arbitrary", "arbitrary"),
                                             vmem_limit_bytes=VMEM_LIMIT_BYTES),
        name="ffn_moe",
    )(*args)


def _combine_kernel(x_ref, ya_ref, yb_ref, route_ref, lng_ref, lnb_ref, out_ref):
    slab = route_ref[...]
    lane = lax.broadcasted_iota(jnp.int32, slab.shape, 1)
    g1 = jnp.sum(jnp.where(lane == LANE_G1, slab, 0.0), axis=-1, keepdims=True)
    g2 = jnp.sum(jnp.where(lane == LANE_G2, slab, 0.0), axis=-1, keepdims=True)
    r = ALPHA * x_ref[...] + (g1 * ya_ref[...] + g2 * yb_ref[...])
    out_ref[...] = _layer_norm(r, lng_ref[...], lnb_ref[...])


def _combine_call(x, ya, yb, route, ln_g, ln_b, *, first_block, n_blocks):
    in_spec = pl.BlockSpec((TB, D_MODEL), lambda t: (first_block + t, 0))
    route_spec = pl.BlockSpec((TB, LANES), lambda t: (first_block + t, 0))
    vec_spec = pl.BlockSpec((1, D_MODEL), lambda t: (0, 0))
    return pl.pallas_call(
        _combine_kernel,
        grid=(n_blocks,),
        in_specs=[in_spec, in_spec, in_spec, route_spec, vec_spec, vec_spec],
        out_specs=pl.BlockSpec((TB, D_MODEL), lambda t: (t, 0)),
        out_shape=jax.ShapeDtypeStruct((n_blocks * TB, D_MODEL), F32),
        compiler_params=pltpu.CompilerParams(dimension_semantics=("arbitrary",),
                                             vmem_limit_bytes=VMEM_LIMIT_BYTES),
        name="moe_combine",
    )(x, ya, yb, route, ln_g, ln_b)


def _take_rows(a, idx):
    return a.at[idx].get(mode="promise_in_bounds")


def _moe_dispatch(route, n_rows):
    i1 = route[:, LANE_I1].astype(jnp.int32)
    i2 = route[:, LANE_I2].astype(jnp.int32)
    experts = jnp.arange(N_EXPERTS, dtype=jnp.int32)
    hit = ((i1[:, None] == experts) | (i2[:, None] == experts)).astype(jnp.int32)
    before = jnp.cumsum(hit, axis=0) - hit
    counts = jnp.sum(hit, axis=0)
    padded = ((counts + TM_MOE - 1) // TM_MOE) * TM_MOE
    ends = jnp.cumsum(padded)
    starts = ends - padded
    rank1 = jnp.sum(jnp.where(i1[:, None] == experts, before, 0), axis=1)
    rank2 = jnp.sum(jnp.where(i2[:, None] == experts, before, 0), axis=1)
    p1 = _take_rows(starts, i1) + rank1
    p2 = _take_rows(starts, i2) + rank2
    n_tiles = (2 * n_rows + N_EXPERTS * (TM_MOE - 1) + TM_MOE - 1) // TM_MOE
    tok = jnp.arange(n_rows, dtype=jnp.int32)
    row_src = jnp.zeros((n_tiles * TM_MOE,), jnp.int32).at[jnp.concatenate([p1, p2])].set(
        jnp.concatenate([tok, tok]), mode="promise_in_bounds", unique_indices=True)
    tile_start = jnp.arange(n_tiles, dtype=jnp.int32) * TM_MOE
    tile_expert = jnp.minimum(jnp.sum(tile_start[:, None] >= ends[None, :], axis=1), N_EXPERTS - 1)
    n_tiles_used = (ends[-1] // TM_MOE).reshape(1)
    return row_src, p1, p2, tile_expert.astype(jnp.int32), n_tiles_used.astype(jnp.int32)


def _split_kernel(a_ref, hi_ref, lo_ref):
    hi, lo = _split_bf16(a_ref[...])
    hi_ref[...] = hi
    lo_ref[...] = lo


def _split_call(a):
    rows, cols = a.shape
    tr = min(rows, 256)
    spec = pl.BlockSpec((tr, cols), lambda r: (r, 0))
    return pl.pallas_call(
        _split_kernel,
        grid=(rows // tr,),
        in_specs=[spec],
        out_specs=[spec, spec],
        out_shape=[jax.ShapeDtypeStruct(a.shape, BF16)] * 2,
        name="split_bf16",
    )(a)


def _pad_w_in(w_in):
    main = w_in[:, :COL_A]
    alr = jnp.pad(w_in[:, COL_A:], ((0, 0), (0, LANES - GATE_RANK)))
    return jnp.concatenate([main, alr], axis=1)


def kernel(x_prompt, x_sample, cache_conv, state_gla, ln_mix_g, ln_mix_b, w_in, w_alpha2, b_alpha,
           conv_w, gla_norm_g, w_out, ln_ffn_g, ln_ffn_b, ffn_w_gate, ffn_w_up, ffn_w_down,
           moe_router, moe_w_gate, moe_w_up, moe_w_down):
    n_prompt_seq, seq_len, _ = x_prompt.shape
    n_sample, dec_len, _ = x_sample.shape
    assert dec_len == CHUNK and n_sample * dec_len == TB and seq_len % TB == 0
    n_prompt_rows = n_prompt_seq * seq_len
    n_rows = n_prompt_rows + TB
    blocks_per_seq = seq_len // TB
    n_prompt_blocks = n_prompt_rows // TB
    assert n_prompt_rows % TM_DENSE == 0

    xp = x_prompt.reshape(n_prompt_rows, D_MODEL)
    xs = x_sample.reshape(TB, D_MODEL)
    gla_p, conv_p, gla_s, conv_s = [], [], [], []
    for l in range(DEPTH):
        moe_layer = l % 2 == 1
        e = l // 2
        w = {
            "w_in": _split_call(_pad_w_in(w_in[l])),
            "wa2": _split_call(jnp.pad(w_alpha2[l], ((0, LANES - GATE_RANK), (0, 0)))),
            "b_alpha": b_alpha[l][None, :],
            "conv_w": conv_w[l],
            "gla_norm_g": gla_norm_g[l][None, :],
            "w_out": _split_call(w_out[l]),
            "ln_g": ln_mix_g[l][None, :],
            "ln_b": ln_mix_b[l][None, :],
        }
        if moe_layer:
            w["router"] = _split_call(jnp.pad(moe_router[e], ((0, 0), (0, LANES - N_EXPERTS))))
        common = dict(n_rows=n_rows, n_blocks=n_prompt_blocks, blocks_per_seq=blocks_per_seq,
                      with_router=moe_layer)
        outs = _mixer_call(xp, w, **common)
        n_stream = 3 if moe_layer else 1
        cp, sp = outs[n_stream:]
        outs_s = _mixer_call(xs, w, sample_state=(cache_conv[l], state_gla[l]), into=outs[:n_stream], **common)
        cs, ss = outs_s[n_stream:]
        conv_p.append(cp)
        gla_p.append(sp)
        conv_s.append(cs)
        gla_s.append(ss)

        ln_g = ln_ffn_g[l][None, :]
        ln_b = ln_ffn_b[l][None, :]
        if moe_layer:
            x1, x1b, route = outs_s[:3]
            row_src, p1, p2, tile_expert, n_tiles_used = _moe_dispatch(route, n_rows)
            n_tiles = row_src.shape[0] // TM_MOE
            bounds = [n_tiles * c // MOE_CALLS for c in range(MOE_CALLS + 1)]
            ys = None
            for t0, t1 in zip(bounds[:-1], bounds[1:]):
                xsort = _take_rows(x1b, row_src[t0 * TM_MOE:t1 * TM_MOE])
                ys = _ffn_moe_call(tile_expert, n_tiles_used, xsort, moe_w_gate, moe_w_up, moe_w_down, e,
                                   first_tile=t0, n_sorted=row_src.shape[0], into=ys)
            comb = functools.partial(_combine_call, x1, _take_rows(ys, p1), _take_rows(ys, p2), route, ln_g, ln_b)
            if l == DEPTH - 1:
                xp = comb(first_block=0, n_blocks=n_prompt_blocks)
                xs = comb(first_block=n_prompt_blocks, n_blocks=1)
            else:
                xp = xs = comb(first_block=0, n_blocks=n_prompt_blocks + 1)
        else:
            x1 = outs_s[0]
            dense = functools.partial(_ffn_dense_call, x1, ffn_w_gate, ffn_w_up, ffn_w_down, e, ln_g, ln_b)
            x2 = dense(tm=TM_DENSE, first_block=0, n_blocks=n_prompt_rows // TM_DENSE, precise=False)
            xp = xs = dense(tm=TB, first_block=n_prompt_blocks, n_blocks=1, precise=(l == 0), into=x2)

    y_prompt = xp[:n_prompt_rows].reshape(n_prompt_seq, seq_len, D_MODEL)
    y_sample = xs[xs.shape[0] - TB:].reshape(n_sample, dec_len, D_MODEL)
    return (y_prompt, y_sample, jnp.stack(gla_p), jnp.stack(conv_p), jnp.stack(gla_s), jnp.stack(conv_s))
```

```python
import functools

import jax
import jax.numpy as jnp
from jax import lax
from jax.experimental import pallas as pl
from jax.experimental.pallas import tpu as pltpu

F32 = jnp.float32
BF16 = jnp.bfloat16

D_MODEL = 1024
DEPTH = 4
CHUNK = 64
CONV_DIM = 512
CONV_W = 3
GLA_HEADS = 4
GLA_HEAD_K = 64
GLA_HEAD_V = 128
GLA_QK = GLA_HEADS * GLA_HEAD_K
GLA_DV = GLA_HEADS * GLA_HEAD_V
GATE_RANK = 16
GATE_TEMP = 16.0
D_FF = 3584
N_EXPERTS = 8
ALPHA = (2 * DEPTH) ** 0.25
LN_EPS = 1e-5

LANES = 128
SUBLANES = 8
VMEM_LIMIT_BYTES = 56 * 1024 * 1024

COL_CB, COL_CC, COL_CH = 0, CONV_DIM, 2 * CONV_DIM
COL_Q = 3 * CONV_DIM
COL_K = COL_Q + GLA_QK
COL_V = COL_K + GLA_QK
COL_G = COL_V + GLA_DV
COL_A = COL_G + GLA_DV
IN_COLS_PAD = COL_A + LANES

TB = 512
SUB = 256
CHUNKS_PER_BLOCK = TB // CHUNK
TF = 512
TM_MOE = 1024
TM_DENSE = 1024
MOE_CALLS = 4
LANE_I1, LANE_I2, LANE_G1, LANE_G2 = 8, 9, 10, 11


def _dot(a, b):
    return jnp.dot(a, b, preferred_element_type=F32)


def _dot_tn(a, b):
    return lax.dot_general(a, b, (((0,), (0,)), ((), ())), preferred_element_type=F32)


def _dot_nt(a, b):
    return lax.dot_general(a, b, (((1,), (1,)), ((), ())), preferred_element_type=F32)


def _split_bf16(a):
    hi = a.astype(BF16)
    lo = (a - hi.astype(F32)).astype(BF16)
    return hi, lo


def _parts(a, precise):
    return _split_bf16(a) if precise else (a.astype(BF16),)


def _mm(a, b, dot=_dot):
    r = dot(a[0], b[0])
    if len(a) > 1:
        r = r + dot(a[1], b[0])
    if len(b) > 1:
        r = r + dot(a[0], b[1])
    return r


def _load(refs, idx=None):
    return tuple(r[...] if idx is None else r[idx] for r in refs)


def _store(refs, idx, parts):
    for r, p in zip(refs, parts):
        r[idx] = p


def _layer_norm(r, g, b):
    mu = jnp.mean(r, axis=-1, keepdims=True)
    d = r - mu
    var = jnp.mean(d * d, axis=-1, keepdims=True)
    return d * lax.rsqrt(var + LN_EPS) * g + b


def _silu(g):
    return g / (1.0 + jnp.exp(-g))


def _mixer_kernel(*refs, sample, precise, blocks_per_seq, with_router, n_alias):
    np_ = 2 if precise else 1
    it = iter(refs)

    def take(n):
        return [next(it) for _ in range(n)]

    x_ref = next(it)
    if not sample:
        xn_ref = next(it)
    if sample:
        convin_ref = next(it)
        statein_ref = next(it)
    w_in_refs = take(np_)
    wa2_refs = take(np_)
    balpha_ref = next(it)
    convw_ref = next(it)
    gnorm_ref = next(it)
    w_out_refs = take(np_)
    lng_ref = next(it)
    lnb_ref = next(it)
    if with_router:
        rhi_ref = next(it)
        rlo_ref = next(it)
    take(n_alias)
    x1_ref = next(it)
    if with_router:
        x1b_ref = next(it)
        route_ref = next(it)
    convout_ref = next(it)
    stateout_ref = next(it)
    proj_ref = next(it)
    ue_ref = next(it)
    prev_ref = next(it)
    fx_ref = next(it)
    y_ref = next(it)
    o_ref = next(it)
    qg_refs = take(np_)
    kd_refs = take(np_)
    v_refs = take(np_)
    la_refs = take(2)
    s_ref = next(it)
    qms_refs = take(np_)
    if not sample:
        xnb_ref = next(it)

    i = pl.program_id(0)
    seq_start = (i % blocks_per_seq) == 0

    def in_proj(src, cols):
        proj_ref[:, cols] = _mm(src, tuple(r[:, cols] for r in w_in_refs))

    if sample:
        in_proj(_parts(x_ref[...], precise), slice(None))
        project_ahead = lambda cols: None
    else:
        @pl.when(i == 0)
        def _():
            in_proj(_parts(x_ref[...], precise), slice(None))

        xnb_ref[...] = xn_ref[...].astype(BF16)
        project_ahead = lambda cols: in_proj((xnb_ref[...],), cols)

    if sample:
        prev_ref[6:8, :] = convin_ref[0]
    else:
        @pl.when(seq_start)
        def _():
            prev_ref[...] = jnp.zeros_like(prev_ref)

    w0 = convw_ref[0:1, :]
    w1 = convw_ref[1:2, :]
    w2 = convw_ref[2:3, :]
    u = proj_ref[:, COL_CC:COL_CC + CONV_DIM] * proj_ref[:, COL_CH:COL_CH + CONV_DIM]
    ue_ref[0:8, :] = prev_ref[...]
    ue_ref[8:8 + TB, :] = u
    conv = ue_ref[6:6 + TB, :] * w0 + ue_ref[7:7 + TB, :] * w1 + u * w2
    y_ref[:, 0:CONV_DIM] = proj_ref[:, COL_CB:COL_CB + CONV_DIM] * conv
    if sample:
        for c in range(CHUNKS_PER_BLOCK):
            r0 = c * CHUNK
            if c > 0:
                fx_ref[c, 6:8, :] = convin_ref[c]
                fx_ref[c, 8:16, :] = ue_ref[8 + r0:16 + r0, :]
                convc = fx_ref[c, 6:14, :] * w0 + fx_ref[c, 7:15, :] * w1 + fx_ref[c, 8:16, :] * w2
                y_ref[r0:r0 + 8, 0:CONV_DIM] = proj_ref[r0:r0 + 8, COL_CB:COL_CB + CONV_DIM] * convc
            convout_ref[c] = ue_ref[8 + r0 + CHUNK - 2:8 + r0 + CHUNK, :]
    else:
        prev_ref[...] = ue_ref[TB:TB + 8, :]
        convout_ref[0] = ue_ref[TB + 6:TB + 8, :]
    project_ahead(slice(COL_CB, COL_Q))

    z = _mm(_parts(proj_ref[:, COL_A:COL_A + LANES], precise), _load(wa2_refs)) + balpha_ref[...]
    la = (jnp.minimum(z, 0.0) - jnp.log(1.0 + jnp.exp(-jnp.abs(z)))) * (1.0 / GATE_TEMP)
    _store(la_refs, slice(None), _split_bf16(la))

    row = lax.broadcasted_iota(jnp.int32, (SUB, SUB), 0)
    col = lax.broadcasted_iota(jnp.int32, (SUB, SUB), 1)
    same_chunk = (row // CHUNK) == (col // CHUNK)
    causal = jnp.logical_and(same_chunk, row >= col)
    tril_bd = (causal.astype(BF16),)
    ones_bd = (same_chunk.astype(BF16),)
    lane_head = lax.broadcasted_iota(jnp.int32, (SUB, GLA_QK), 1) // GLA_HEAD_K

    for sb in range(TB // SUB):
        rs = slice(sb * SUB, (sb + 1) * SUB)
        lap = _load(la_refs, (rs, slice(None)))
        bcum = _mm(tril_bd, lap)
        blast = _mm(ones_bd, lap)
        q = proj_ref[rs, COL_Q:COL_Q + GLA_QK]
        k = proj_ref[rs, COL_K:COL_K + GLA_QK]
        qg = q * jnp.exp(bcum) * (GLA_HEAD_K ** -0.5)
        kg = _parts(k * jnp.exp(-bcum), precise)
        _store(qg_refs, (rs, slice(None)), _parts(qg, precise))
        _store(kd_refs, (rs, slice(None)), _parts(k * jnp.exp(blast - bcum), precise))
        vp = _parts(proj_ref[rs, COL_V:COL_V + GLA_DV], precise)
        _store(v_refs, (rs, slice(None)), vp)
        for h in range(GLA_HEADS):
            vs = slice(h * GLA_HEAD_V, (h + 1) * GLA_HEAD_V)
            qm = _parts(jnp.where(lane_head == h, qg, 0.0), precise)
            for cc in range(SUB // CHUNK):
                dst = (sb * (SUB // CHUNK) + cc, slice(h * CHUNK, (h + 1) * CHUNK), slice(None))
                _store(qms_refs, dst, tuple(p[cc * CHUNK:(cc + 1) * CHUNK] for p in qm))
            sc = jnp.where(causal, _mm(qm, kg, _dot_nt), 0.0)
            o_ref[rs, vs] = _mm(_parts(sc, precise), tuple(p[:, vs] for p in vp))

    project_ahead(slice(COL_Q, COL_G))
    project_ahead(slice(COL_A, IN_COLS_PAD))
    trow = lax.broadcasted_iota(jnp.int32, (TB, LANES), 0) // CHUNK
    tcol = lax.broadcasted_iota(jnp.int32, (TB, LANES), 1)
    chunk_sel = ((trow == tcol).astype(BF16),)
    dec_all = jnp.exp(_mm(_load(la_refs), chunk_sel, _dot_tn))
    if not sample:
        s = jnp.where(seq_start, 0.0, s_ref[...])
    for c in range(CHUNKS_PER_BLOCK):
        cs = slice(c * CHUNK, (c + 1) * CHUNK)
        if sample:
            s = statein_ref[c].reshape(GLA_QK, GLA_HEAD_V)
        oi = _mm(_load(qms_refs, c), _parts(s, precise))
        for h in range(GLA_HEADS):
            o_ref[cs, h * GLA_HEAD_V:(h + 1) * GLA_HEAD_V] += oi[h * CHUNK:(h + 1) * CHUNK, :]
        ds = _mm(_load(kd_refs, (cs, slice(None))), _load(v_refs, (cs, slice(None))), _dot_tn)
        ds_diag = jnp.concatenate([ds[h * GLA_HEAD_K:(h + 1) * GLA_HEAD_K, h * GLA_HEAD_V:(h + 1) * GLA_HEAD_V]
                                   for h in range(GLA_HEADS)], axis=0)
        s = dec_all[:, c:c + 1] * s + ds_diag
        if sample:
            stateout_ref[c] = s.reshape(GLA_HEADS, GLA_HEAD_K, GLA_HEAD_V)
    if not sample:
        s_ref[...] = s
        stateout_ref[0] = s.reshape(GLA_HEADS, GLA_HEAD_K, GLA_HEAD_V)

    gn = gnorm_ref[...]
    for h in range(GLA_HEADS):
        vs = slice(h * GLA_HEAD_V, (h + 1) * GLA_HEAD_V)
        oh = o_ref[:, vs]
        ms = jnp.mean(oh * oh, axis=-1, keepdims=True)
        g = proj_ref[:, COL_G + h * GLA_HEAD_V:COL_G + (h + 1) * GLA_HEAD_V]
        y_ref[:, CONV_DIM + h * GLA_HEAD_V:CONV_DIM + (h + 1) * GLA_HEAD_V] = (
            oh * lax.rsqrt(ms + LN_EPS) * gn * _silu(g))

    project_ahead(slice(COL_G, COL_A))

    m = _mm(_parts(y_ref[...], precise), _load(w_out_refs))
    x1 = _layer_norm(ALPHA * x_ref[...] + m, lng_ref[...], lnb_ref[...])
    x1_ref[...] = x1

    if with_router:
        x1b_ref[...] = x1.astype(BF16)
        lg = _mm(_split_bf16(x1), (rhi_ref[...], rlo_ref[...]))
        lane = lax.broadcasted_iota(jnp.int32, lg.shape, 1)
        neg = jnp.float32(-jnp.inf)
        lg = jnp.where(lane < N_EXPERTS, lg, neg)
        m1 = jnp.max(lg, axis=-1, keepdims=True)
        i1 = jnp.min(jnp.where(lg == m1, lane, LANES), axis=-1, keepdims=True)
        lg2 = jnp.where(lane == i1, neg, lg)
        m2 = jnp.max(lg2, axis=-1, keepdims=True)
        i2 = jnp.min(jnp.where(lg2 == m2, lane, LANES), axis=-1, keepdims=True)
        e2 = jnp.exp(m2 - m1)
        g1 = 1.0 / (1.0 + e2)
        g2 = e2 / (1.0 + e2)
        slab = jnp.where(lane == i1, g1, jnp.where(lane == i2, g2, 0.0))
        slab = jnp.where(lane == LANE_I1, i1.astype(F32), slab)
        slab = jnp.where(lane == LANE_I2, i2.astype(F32), slab)
        slab = jnp.where(lane == LANE_G1, g1, slab)
        slab = jnp.where(lane == LANE_G2, g2, slab)
        route_ref[...] = slab


def _mixer_call(x, w, *, n_rows, n_blocks, blocks_per_seq, with_router, sample_state=None, into=None):
    sample = sample_state is not None
    precise = sample
    np_ = 2 if precise else 1
    const2 = lambda i: (0, 0)
    const3 = lambda i: (0, 0, 0)
    const4 = lambda i: (0, 0, 0, 0)

    x_last = x.shape[0] // TB - 1
    in_specs = [pl.BlockSpec((TB, D_MODEL), (lambda i: (x_last, 0)) if sample else (lambda i: (i, 0)))]
    args = [x]
    if not sample:
        in_specs.append(pl.BlockSpec((TB, D_MODEL), lambda i: (jnp.minimum(i + 1, n_blocks - 1), 0)))
        args.append(x)
    if sample:
        conv_in, state_in = sample_state
        in_specs += [pl.BlockSpec(conv_in.shape, const3), pl.BlockSpec(state_in.shape, const4)]
        args += [conv_in, state_in]
    in_specs += [pl.BlockSpec((D_MODEL, IN_COLS_PAD), const2)] * np_
    args += list(w["w_in"][:np_])
    in_specs += [pl.BlockSpec((LANES, GLA_QK), const2)] * np_
    args += list(w["wa2"][:np_])
    in_specs += [pl.BlockSpec((1, GLA_QK), const2), pl.BlockSpec((CONV_W, CONV_DIM), const2),
                 pl.BlockSpec((1, GLA_HEAD_V), const2)]
    args += [w["b_alpha"], w["conv_w"], w["gla_norm_g"]]
    in_specs += [pl.BlockSpec((D_MODEL, D_MODEL), const2)] * np_
    args += list(w["w_out"][:np_])
    in_specs += [pl.BlockSpec((1, D_MODEL), const2)] * 2
    args += [w["ln_g"], w["ln_b"]]
    if with_router:
        in_specs += [pl.BlockSpec((D_MODEL, LANES), const2)] * 2
        args += list(w["router"])
    aliases = {}
    n_alias = 0
    if sample:
        n_alias = len(into)
        for k, buf in enumerate(into):
            aliases[len(args)] = k
            in_specs.append(pl.BlockSpec(memory_space=pl.ANY))
            args.append(buf)

    out_row = (lambda i: (n_blocks, 0)) if sample else (lambda i: (i, 0))
    out_shape = [jax.ShapeDtypeStruct((n_rows, D_MODEL), F32)]
    out_specs = [pl.BlockSpec((TB, D_MODEL), out_row)]
    if with_router:
        out_shape += [jax.ShapeDtypeStruct((n_rows, D_MODEL), BF16),
                      jax.ShapeDtypeStruct((n_rows, LANES), F32)]
        out_specs += [pl.BlockSpec((TB, D_MODEL), out_row), pl.BlockSpec((TB, LANES), out_row)]
    if sample:
        out_shape += [jax.ShapeDtypeStruct(conv_in.shape, F32), jax.ShapeDtypeStruct(state_in.shape, F32)]
        out_specs += [pl.BlockSpec(conv_in.shape, const3), pl.BlockSpec(state_in.shape, const4)]
    else:
        n_seq = n_blocks // blocks_per_seq
        out_shape += [jax.ShapeDtypeStruct((n_seq, CONV_W - 1, CONV_DIM), F32),
                      jax.ShapeDtypeStruct((n_seq, GLA_HEADS, GLA_HEAD_K, GLA_HEAD_V), F32)]
        out_specs += [pl.BlockSpec((1, CONV_W - 1, CONV_DIM), lambda i: (i // blocks_per_seq, 0, 0)),
                      pl.BlockSpec((1, GLA_HEADS, GLA_HEAD_K, GLA_HEAD_V),
                                   lambda i: (i // blocks_per_seq, 0, 0, 0))]
    scratch = [
        pltpu.VMEM((TB, IN_COLS_PAD), F32),
        pltpu.VMEM((TB + 8, CONV_DIM), F32),
        pltpu.VMEM((8, CONV_DIM), F32),
        pltpu.VMEM((CHUNKS_PER_BLOCK, 16, CONV_DIM), F32),
        pltpu.VMEM((TB, D_MODEL), F32),
        pltpu.VMEM((TB, GLA_DV), F32),
    ]
    scratch += [pltpu.VMEM((TB, GLA_QK), BF16)] * np_
    scratch += [pltpu.VMEM((TB, GLA_QK), BF16)] * np_
    scratch += [pltpu.VMEM((TB, GLA_DV), BF16)] * np_
    scratch += [pltpu.VMEM((TB, GLA_QK), BF16)] * 2
    scratch += [pltpu.VMEM((GLA_QK, GLA_HEAD_V), F32)]
    scratch += [pltpu.VMEM((CHUNKS_PER_BLOCK, GLA_HEADS * CHUNK, GLA_QK), BF16)] * np_
    if not sample:
        scratch += [pltpu.VMEM((TB, D_MODEL), BF16)]
    kern = functools.partial(_mixer_kernel, sample=sample, precise=precise, blocks_per_seq=blocks_per_seq,
                             with_router=with_router, n_alias=n_alias)
    name = ("mixer_sample" if sample else "mixer_prompt") + ("_router" if with_router else "")
    return pl.pallas_call(
        kern,
        grid=(1 if sample else n_blocks,),
        in_specs=in_specs,
        out_specs=out_specs,
        out_shape=out_shape,
        scratch_shapes=scratch,
        input_output_aliases=aliases,
        compiler_params=pltpu.CompilerParams(dimension_semantics=("arbitrary",),
                                             vmem_limit_bytes=VMEM_LIMIT_BYTES),
        name=name,
    )(*args)


def _swiglu_step(xp, wg, wu, wd, acc_ref, precise):
    f = pl.program_id(1)
    g = _mm(xp, _parts(wg, precise))
    u = _mm(xp, _parts(wu, precise))
    part = _mm(_parts(_silu(g) * u, precise), _parts(wd, precise))

    @pl.when(f == 0)
    def _():
        acc_ref[...] = part

    @pl.when(f > 0)
    def _():
        acc_ref[...] += part


def _ffn_dense_kernel(*refs, precise, n_alias):
    x_ref, wg_ref, wu_ref, wd_ref, lng_ref, lnb_ref = refs[:6]
    out_ref, acc_ref = refs[6 + n_alias:]
    x = x_ref[...]
    _swiglu_step(_parts(x, precise), wg_ref[0], wu_ref[0], wd_ref[0], acc_ref, precise)

    @pl.when(pl.program_id(1) == pl.num_programs(1) - 1)
    def _():
        out_ref[...] = _layer_norm(ALPHA * x + acc_ref[...], lng_ref[...], lnb_ref[...])


def _ffn_dense_call(x, wg, wu, wd, e, ln_g, ln_b, *, tm, first_block, n_blocks, precise, into=None):
    n_rows = x.shape[0]
    nf = D_FF // TF
    row_spec = pl.BlockSpec((tm, D_MODEL), lambda t, f: (first_block + t, 0))
    in_specs = [row_spec,
                pl.BlockSpec((1, D_MODEL, TF), lambda t, f: (e, 0, f)),
                pl.BlockSpec((1, D_MODEL, TF), lambda t, f: (e, 0, f)),
                pl.BlockSpec((1, TF, D_MODEL), lambda t, f: (e, f, 0)),
                pl.BlockSpec((1, D_MODEL), lambda t, f: (0, 0)),
                pl.BlockSpec((1, D_MODEL), lambda t, f: (0, 0))]
    args = [x, wg, wu, wd, ln_g, ln_b]
    aliases = {}
    n_alias = 0
    if into is not None:
        n_alias = 1
        aliases[len(args)] = 0
        in_specs.append(pl.BlockSpec(memory_space=pl.ANY))
        args.append(into)
    return pl.pallas_call(
        functools.partial(_ffn_dense_kernel, precise=precise, n_alias=n_alias),
        grid=(n_blocks, nf),
        in_specs=in_specs,
        out_specs=row_spec,
        out_shape=jax.ShapeDtypeStruct((n_rows, D_MODEL), F32),
        scratch_shapes=[pltpu.VMEM((tm, D_MODEL), F32)],
        input_output_aliases=aliases,
        compiler_params=pltpu.CompilerParams(dimension_semantics=("arbitrary", "arbitrary"),
                                             vmem_limit_bytes=VMEM_LIMIT_BYTES),
        name="ffn_dense_precise" if precise else "ffn_dense",
    )(*args)


def _ffn_moe_kernel(te_ref, nt_ref, x_ref, wg_ref, wu_ref, wd_ref, *rest, first_tile, n_alias):
    out_ref, acc_ref = rest[n_alias:]
    used = first_tile + pl.program_id(0) < nt_ref[0]

    @pl.when(used)
    def _():
        _swiglu_step((x_ref[...],), wg_ref[0, 0], wu_ref[0, 0], wd_ref[0, 0], acc_ref, False)

    last = pl.program_id(1) == pl.num_programs(1) - 1

    @pl.when(jnp.logical_and(last, used))
    def _():
        out_ref[...] = acc_ref[...]

    @pl.when(jnp.logical_and(last, jnp.logical_not(used)))
    def _():
        out_ref[...] = jnp.zeros_like(out_ref)


def _ffn_moe_call(tile_expert, n_tiles_used, xs, wg, wu, wd, e, *, first_tile, n_sorted, into=None):
    nf = D_FF // TF
    in_specs = [
        pl.BlockSpec((TM_MOE, D_MODEL), lambda t, f, te, nt: (t, 0)),
        pl.BlockSpec((1, 1, D_MODEL, TF), lambda t, f, te, nt: (e, te[first_tile + t], 0, f)),
        pl.BlockSpec((1, 1, D_MODEL, TF), lambda t, f, te, nt: (e, te[first_tile + t], 0, f)),
        pl.BlockSpec((1, 1, TF, D_MODEL), lambda t, f, te, nt: (e, te[first_tile + t], f, 0)),
    ]
    args = [tile_expert, n_tiles_used, xs, wg, wu, wd]
    aliases = {}
    n_alias = 0
    if into is not None:
        n_alias = 1
        aliases[len(args)] = 0
        in_specs.append(pl.BlockSpec(memory_space=pl.ANY))
        args.append(into)
    grid_spec = pltpu.PrefetchScalarGridSpec(
        num_scalar_prefetch=2,
        grid=(xs.shape[0] // TM_MOE, nf),
        in_specs=in_specs,
        out_specs=pl.BlockSpec((TM_MOE, D_MODEL), lambda t, f, te, nt: (first_tile + t, 0)),
        scratch_shapes=[pltpu.VMEM((TM_MOE, D_MODEL), F32)],
    )
    return pl.pallas_call(
        functools.partial(_ffn_moe_kernel, first_tile=first_tile, n_alias=n_alias),
        grid_spec=grid_spec,
        out_shape=jax.ShapeDtypeStruct((n_sorted, D_MODEL), F32),
        input_output_aliases=aliases,
        compiler_params=pltpu.CompilerParams(dimension_semantics=("arbitrary", "arbitrary"),
                                             vmem_limit_bytes=VMEM_LIMIT_BYTES),
        name="ffn_moe",
    )(*args)


def _combine_kernel(x_ref, ya_ref, yb_ref, route_ref, lng_ref, lnb_ref, out_ref):
    slab = route_ref[...]
    lane = lax.broadcasted_iota(jnp.int32, slab.shape, 1)
    g1 = jnp.sum(jnp.where(lane == LANE_G1, slab, 0.0), axis=-1, keepdims=True)
    g2 = jnp.sum(jnp.where(lane == LANE_G2, slab, 0.0), axis=-1, keepdims=True)
    r = ALPHA * x_ref[...] + (g1 * ya_ref[...] + g2 * yb_ref[...])
    out_ref[...] = _layer_norm(r, lng_ref[...], lnb_ref[...])


def _combine_call(x, ya, yb, route, ln_g, ln_b, *, first_block, n_blocks):
    in_spec = pl.BlockSpec((TB, D_MODEL), lambda t: (first_block + t, 0))
    route_spec = pl.BlockSpec((TB, LANES), lambda t: (first_block + t, 0))
    vec_spec = pl.BlockSpec((1, D_MODEL), lambda t: (0, 0))
    return pl.pallas_call(
        _combine_kernel,
        grid=(n_blocks,),
        in_specs=[in_spec, in_spec, in_spec, route_spec, vec_spec, vec_spec],
        out_specs=pl.BlockSpec((TB, D_MODEL), lambda t: (t, 0)),
        out_shape=jax.ShapeDtypeStruct((n_blocks * TB, D_MODEL), F32),
        compiler_params=pltpu.CompilerParams(dimension_semantics=("arbitrary",),
                                             vmem_limit_bytes=VMEM_LIMIT_BYTES),
        name="moe_combine",
    )(x, ya, yb, route, ln_g, ln_b)


def _take_rows(a, idx):
    return a.at[idx].get(mode="promise_in_bounds")


def _moe_dispatch(route, n_rows):
    i1 = route[:, LANE_I1].astype(jnp.int32)
    i2 = route[:, LANE_I2].astype(jnp.int32)
    experts = jnp.arange(N_EXPERTS, dtype=jnp.int32)
    hit = ((i1[:, None] == experts) | (i2[:, None] == experts)).astype(jnp.int32)
    before = jnp.cumsum(hit, axis=0) - hit
    counts = jnp.sum(hit, axis=0)
    padded = ((counts + TM_MOE - 1) // TM_MOE) * TM_MOE
    ends = jnp.cumsum(padded)
    starts = ends - padded
    rank1 = jnp.sum(jnp.where(i1[:, None] == experts, before, 0), axis=1)
    rank2 = jnp.sum(jnp.where(i2[:, None] == experts, before, 0), axis=1)
    p1 = _take_rows(starts, i1) + rank1
    p2 = _take_rows(starts, i2) + rank2
    n_tiles = (2 * n_rows + N_EXPERTS * (TM_MOE - 1) + TM_MOE - 1) // TM_MOE
    tok = jnp.arange(n_rows, dtype=jnp.int32)
    row_src = jnp.zeros((n_tiles * TM_MOE,), jnp.int32).at[jnp.concatenate([p1, p2])].set(
        jnp.concatenate([tok, tok]), mode="promise_in_bounds", unique_indices=True)
    tile_start = jnp.arange(n_tiles, dtype=jnp.int32) * TM_MOE
    tile_expert = jnp.minimum(jnp.sum(tile_start[:, None] >= ends[None, :], axis=1), N_EXPERTS - 1)
    n_tiles_used = (ends[-1] // TM_MOE).reshape(1)
    return row_src, p1, p2, tile_expert.astype(jnp.int32), n_tiles_used.astype(jnp.int32)


def _split_kernel(a_ref, hi_ref, lo_ref):
    hi, lo = _split_bf16(a_ref[...])
    hi_ref[...] = hi
    lo_ref[...] = lo


def _split_call(a):
    rows, cols = a.shape
    tr = min(rows, 256)
    spec = pl.BlockSpec((tr, cols), lambda r: (r, 0))
    return pl.pallas_call(
        _split_kernel,
        grid=(rows // tr,),
        in_specs=[spec],
        out_specs=[spec, spec],
        out_shape=[jax.ShapeDtypeStruct(a.shape, BF16)] * 2,
        name="split_bf16",
    )(a)


def _pad_w_in(w_in):
    main = w_in[:, :COL_A]
    alr = jnp.pad(w_in[:, COL_A:], ((0, 0), (0, LANES - GATE_RANK)))
    return jnp.concatenate([main, alr], axis=1)


def kernel(x_prompt, x_sample, cache_conv, state_gla, ln_mix_g, ln_mix_b, w_in, w_alpha2, b_alpha,
           conv_w, gla_norm_g, w_out, ln_ffn_g, ln_ffn_b, ffn_w_gate, ffn_w_up, ffn_w_down,
           moe_router, moe_w_gate, moe_w_up, moe_w_down):
    n_prompt_seq, seq_len, _ = x_prompt.shape
    n_sample, dec_len, _ = x_sample.shape
    assert dec_len == CHUNK and n_sample * dec_len == TB and seq_len % TB == 0
    n_prompt_rows = n_prompt_seq * seq_len
    n_rows = n_prompt_rows + TB
    blocks_per_seq = seq_len // TB
    n_prompt_blocks = n_prompt_rows // TB
    assert n_prompt_rows % TM_DENSE == 0

    xp = x_prompt.reshape(n_prompt_rows, D_MODEL)
    xs = x_sample.reshape(TB, D_MODEL)
    gla_p, conv_p, gla_s, conv_s = [], [], [], []
    for l in range(DEPTH):
        moe_layer = l % 2 == 1
        e = l // 2
        w = {
            "w_in": _split_call(_pad_w_in(w_in[l])),
            "wa2": _split_call(jnp.pad(w_alpha2[l], ((0, LANES - GATE_RANK), (0, 0)))),
            "b_alpha": b_alpha[l][None, :],
            "conv_w": conv_w[l],
            "gla_norm_g": gla_norm_g[l][None, :],
            "w_out": _split_call(w_out[l]),
            "ln_g": ln_mix_g[l][None, :],
            "ln_b": ln_mix_b[l][None, :],
        }
        if moe_layer:
            w["router"] = _split_call(jnp.pad(moe_router[e], ((0, 0), (0, LANES - N_EXPERTS))))
        common = dict(n_rows=n_rows, n_blocks=n_prompt_blocks, blocks_per_seq=blocks_per_seq,
                      with_router=moe_layer)
        outs = _mixer_call(xp, w, **common)
        n_stream = 3 if moe_layer else 1
        cp, sp = outs[n_stream:]
        outs_s = _mixer_call(xs, w, sample_state=(cache_conv[l], state_gla[l]), into=outs[:n_stream], **common)
        cs, ss = outs_s[n_stream:]
        conv_p.append(cp)
        gla_p.append(sp)
        conv_s.append(cs)
        gla_s.append(ss)

        ln_g = ln_ffn_g[l][None, :]
        ln_b = ln_ffn_b[l][None, :]
        if moe_layer:
            x1, x1b, route = outs_s[:3]
            row_src, p1, p2, tile_expert, n_tiles_used = _moe_dispatch(route, n_rows)
            n_tiles = row_src.shape[0] // TM_MOE
            bounds = [n_tiles * c // MOE_CALLS for c in range(MOE_CALLS + 1)]
            ys = None
            for t0, t1 in zip(bounds[:-1], bounds[1:]):
                xsort = _take_rows(x1b, row_src[t0 * TM_MOE:t1 * TM_MOE])
                ys = _ffn_moe_call(tile_expert, n_tiles_used, xsort, moe_w_gate, moe_w_up, moe_w_down, e,
                                   first_tile=t0, n_sorted=row_src.shape[0], into=ys)
            comb = functools.partial(_combine_call, x1, _take_rows(ys, p1), _take_rows(ys, p2), route, ln_g, ln_b)
            if l == DEPTH - 1:
                xp = comb(first_block=0, n_blocks=n_prompt_blocks)
                xs = comb(first_block=n_prompt_blocks, n_blocks=1)
            else:
                xp = xs = comb(first_block=0, n_blocks=n_prompt_blocks + 1)
        else:
            x1 = outs_s[0]
            dense = functools.partial(_ffn_dense_call, x1, ffn_w_gate, ffn_w_up, ffn_w_down, e, ln_g, ln_b)
            x2 = dense(tm=TM_DENSE, first_block=0, n_blocks=n_prompt_rows // TM_DENSE, precise=False)
            xp = xs = dense(tm=TB, first_block=n_prompt_blocks, n_blocks=1, precise=(l == 0), into=x2)

    y_prompt = xp[:n_prompt_rows].reshape(n_prompt_seq, seq_len, D_MODEL)
    y_sample = xs[xs.shape[0] - TB:].reshape(n_sample, dec_len, D_MODEL)
    return (y_prompt, y_sample, jnp.stack(gla_p), jnp.stack(conv_p), jnp.stack(gla_s), jnp.stack(conv_s))
```

```python
import functools

import jax
import jax.numpy as jnp
from jax import lax
from jax.experimental import pallas as pl
from jax.experimental.pallas import tpu as pltpu

F32 = jnp.float32
BF16 = jnp.bfloat16

D_MODEL = 1024
DEPTH = 4
CHUNK = 64
CONV_DIM = 512
CONV_W = 3
GLA_HEADS = 4
GLA_HEAD_K = 64
GLA_HEAD_V = 128
GLA_QK = GLA_HEADS * GLA_HEAD_K
GLA_DV = GLA_HEADS * GLA_HEAD_V
GATE_RANK = 16
GATE_TEMP = 16.0
D_FF = 3584
N_EXPERTS = 8
ALPHA = (2 * DEPTH) ** 0.25
LN_EPS = 1e-5

LANES = 128
SUBLANES = 8
VMEM_LIMIT_BYTES = 56 * 1024 * 1024

COL_CB, COL_CC, COL_CH = 0, CONV_DIM, 2 * CONV_DIM
COL_Q = 3 * CONV_DIM
COL_K = COL_Q + GLA_QK
COL_V = COL_K + GLA_QK
COL_G = COL_V + GLA_DV
COL_A = COL_G + GLA_DV
IN_COLS_PAD = COL_A + LANES

TB = 512
SUB = 256
CHUNKS_PER_BLOCK = TB // CHUNK
TF = 512
TM_MOE = 1024
TM_DENSE = 1024
MOE_CALLS = 4
ROUTE_ROWS = 8
ROW_I1, ROW_I2, ROW_G1, ROW_G2 = 0, 1, 2, 3


def _dot(a, b):
    return jnp.dot(a, b, preferred_element_type=F32)


def _dot_tn(a, b):
    return lax.dot_general(a, b, (((0,), (0,)), ((), ())), preferred_element_type=F32)


def _dot_nt(a, b):
    return lax.dot_general(a, b, (((1,), (1,)), ((), ())), preferred_element_type=F32)


def _split_bf16(a):
    hi = a.astype(BF16)
    lo = (a - hi.astype(F32)).astype(BF16)
    return hi, lo


def _parts(a, precise):
    return _split_bf16(a) if precise else (a.astype(BF16),)


def _mm(a, b, dot=_dot):
    r = dot(a[0], b[0])
    if len(a) > 1:
        r = r + dot(a[1], b[0])
    if len(b) > 1:
        r = r + dot(a[0], b[1])
    return r


def _load(refs, idx=None):
    return tuple(r[...] if idx is None else r[idx] for r in refs)


def _store(refs, idx, parts):
    for r, p in zip(refs, parts):
        r[idx] = p


def _layer_norm(r, g, b):
    mu = jnp.mean(r, axis=-1, keepdims=True)
    d = r - mu
    var = jnp.mean(d * d, axis=-1, keepdims=True)
    return d * lax.rsqrt(var + LN_EPS) * g + b


def _silu(g):
    return g / (1.0 + jnp.exp(-g))


def _mixer_kernel(*refs, sample, precise, blocks_per_seq, with_router, n_alias):
    np_ = 2 if precise else 1
    it = iter(refs)

    def take(n):
        return [next(it) for _ in range(n)]

    x_ref = next(it)
    if not sample:
        xn_ref = next(it)
    if sample:
        convin_ref = next(it)
        statein_ref = next(it)
    w_in_refs = take(np_)
    wa2_refs = take(np_)
    balpha_ref = next(it)
    convw_ref = next(it)
    gnorm_ref = next(it)
    w_out_refs = take(np_)
    lng_ref = next(it)
    lnb_ref = next(it)
    if with_router:
        rhi_ref = next(it)
        rlo_ref = next(it)
    take(n_alias)
    x1_ref = next(it)
    if with_router:
        x1b_ref = next(it)
        route_ref = next(it)
    convout_ref = next(it)
    stateout_ref = next(it)
    proj_ref = next(it)
    ue_ref = next(it)
    prev_ref = next(it)
    fx_ref = next(it)
    y_ref = next(it)
    o_ref = next(it)
    qg_refs = take(np_)
    kd_refs = take(np_)
    v_refs = take(np_)
    la_refs = take(2)
    s_ref = next(it)
    qms_refs = take(np_)
    if not sample:
        xnb_ref = next(it)

    i = pl.program_id(0)
    seq_start = (i % blocks_per_seq) == 0

    def in_proj(src, cols):
        proj_ref[:, cols] = _mm(src, tuple(r[:, cols] for r in w_in_refs))

    if sample:
        in_proj(_parts(x_ref[...], precise), slice(None))
        project_ahead = lambda cols: None
    else:
        @pl.when(i == 0)
        def _():
            in_proj(_parts(x_ref[...], precise), slice(None))

        xnb_ref[...] = xn_ref[...].astype(BF16)
        project_ahead = lambda cols: in_proj((xnb_ref[...],), cols)

    if sample:
        prev_ref[6:8, :] = convin_ref[0]
    else:
        @pl.when(seq_start)
        def _():
            prev_ref[...] = jnp.zeros_like(prev_ref)

    w0 = convw_ref[0:1, :]
    w1 = convw_ref[1:2, :]
    w2 = convw_ref[2:3, :]
    u = proj_ref[:, COL_CC:COL_CC + CONV_DIM] * proj_ref[:, COL_CH:COL_CH + CONV_DIM]
    ue_ref[0:8, :] = prev_ref[...]
    ue_ref[8:8 + TB, :] = u
    conv = ue_ref[6:6 + TB, :] * w0 + ue_ref[7:7 + TB, :] * w1 + u * w2
    y_ref[:, 0:CONV_DIM] = proj_ref[:, COL_CB:COL_CB + CONV_DIM] * conv
    if sample:
        for c in range(CHUNKS_PER_BLOCK):
            r0 = c * CHUNK
            if c > 0:
                fx_ref[c, 6:8, :] = convin_ref[c]
                fx_ref[c, 8:16, :] = ue_ref[8 + r0:16 + r0, :]
                convc = fx_ref[c, 6:14, :] * w0 + fx_ref[c, 7:15, :] * w1 + fx_ref[c, 8:16, :] * w2
                y_ref[r0:r0 + 8, 0:CONV_DIM] = proj_ref[r0:r0 + 8, COL_CB:COL_CB + CONV_DIM] * convc
            convout_ref[c] = ue_ref[8 + r0 + CHUNK - 2:8 + r0 + CHUNK, :]
    else:
        prev_ref[...] = ue_ref[TB:TB + 8, :]
        convout_ref[0] = ue_ref[TB + 6:TB + 8, :]
    project_ahead(slice(COL_CB, COL_Q))

    z = _mm(_parts(proj_ref[:, COL_A:COL_A + LANES], precise), _load(wa2_refs)) + balpha_ref[...]
    la = (jnp.minimum(z, 0.0) - jnp.log(1.0 + jnp.exp(-jnp.abs(z)))) * (1.0 / GATE_TEMP)
    _store(la_refs, slice(None), _split_bf16(la))

    row = lax.broadcasted_iota(jnp.int32, (SUB, SUB), 0)
    col = lax.broadcasted_iota(jnp.int32, (SUB, SUB), 1)
    same_chunk = (row // CHUNK) == (col // CHUNK)
    causal = jnp.logical_and(same_chunk, row >= col)
    tril_bd = (causal.astype(BF16),)
    ones_bd = (same_chunk.astype(BF16),)
    lane_head = lax.broadcasted_iota(jnp.int32, (SUB, GLA_QK), 1) // GLA_HEAD_K

    for sb in range(TB // SUB):
        rs = slice(sb * SUB, (sb + 1) * SUB)
        lap = _load(la_refs, (rs, slice(None)))
        bcum = _mm(tril_bd, lap)
        blast = _mm(ones_bd, lap)
        q = proj_ref[rs, COL_Q:COL_Q + GLA_QK]
        k = proj_ref[rs, COL_K:COL_K + GLA_QK]
        qg = q * jnp.exp(bcum) * (GLA_HEAD_K ** -0.5)
        kg = _parts(k * jnp.exp(-bcum), precise)
        _store(qg_refs, (rs, slice(None)), _parts(qg, precise))
        _store(kd_refs, (rs, slice(None)), _parts(k * jnp.exp(blast - bcum), precise))
        vp = _parts(proj_ref[rs, COL_V:COL_V + GLA_DV], precise)
        _store(v_refs, (rs, slice(None)), vp)
        for h in range(GLA_HEADS):
            vs = slice(h * GLA_HEAD_V, (h + 1) * GLA_HEAD_V)
            qm = _parts(jnp.where(lane_head == h, qg, 0.0), precise)
            for cc in range(SUB // CHUNK):
                dst = (sb * (SUB // CHUNK) + cc, slice(h * CHUNK, (h + 1) * CHUNK), slice(None))
                _store(qms_refs, dst, tuple(p[cc * CHUNK:(cc + 1) * CHUNK] for p in qm))
            sc = jnp.where(causal, _mm(qm, kg, _dot_nt), 0.0)
            o_ref[rs, vs] = _mm(_parts(sc, precise), tuple(p[:, vs] for p in vp))

    project_ahead(slice(COL_Q, COL_G))
    project_ahead(slice(COL_A, IN_COLS_PAD))
    trow = lax.broadcasted_iota(jnp.int32, (TB, LANES), 0) // CHUNK
    tcol = lax.broadcasted_iota(jnp.int32, (TB, LANES), 1)
    chunk_sel = ((trow == tcol).astype(BF16),)
    dec_all = jnp.exp(_mm(_load(la_refs), chunk_sel, _dot_tn))
    if not sample:
        s = jnp.where(seq_start, 0.0, s_ref[...])
    for c in range(CHUNKS_PER_BLOCK):
        cs = slice(c * CHUNK, (c + 1) * CHUNK)
        if sample:
            s = statein_ref[c].reshape(GLA_QK, GLA_HEAD_V)
        oi = _mm(_load(qms_refs, c), _parts(s, precise))
        for h in range(GLA_HEADS):
            o_ref[cs, h * GLA_HEAD_V:(h + 1) * GLA_HEAD_V] += oi[h * CHUNK:(h + 1) * CHUNK, :]
        ds = _mm(_load(kd_refs, (cs, slice(None))), _load(v_refs, (cs, slice(None))), _dot_tn)
        ds_diag = jnp.concatenate([ds[h * GLA_HEAD_K:(h + 1) * GLA_HEAD_K, h * GLA_HEAD_V:(h + 1) * GLA_HEAD_V]
                                   for h in range(GLA_HEADS)], axis=0)
        s = dec_all[:, c:c + 1] * s + ds_diag
        if sample:
            stateout_ref[c] = s.reshape(GLA_HEADS, GLA_HEAD_K, GLA_HEAD_V)
    if not sample:
        s_ref[...] = s
        stateout_ref[0] = s.reshape(GLA_HEADS, GLA_HEAD_K, GLA_HEAD_V)

    gn = gnorm_ref[...]
    for h in range(GLA_HEADS):
        vs = slice(h * GLA_HEAD_V, (h + 1) * GLA_HEAD_V)
        oh = o_ref[:, vs]
        ms = jnp.mean(oh * oh, axis=-1, keepdims=True)
        g = proj_ref[:, COL_G + h * GLA_HEAD_V:COL_G + (h + 1) * GLA_HEAD_V]
        y_ref[:, CONV_DIM + h * GLA_HEAD_V:CONV_DIM + (h + 1) * GLA_HEAD_V] = (
            oh * lax.rsqrt(ms + LN_EPS) * gn * _silu(g))

    project_ahead(slice(COL_G, COL_A))

    m = _mm(_parts(y_ref[...], precise), _load(w_out_refs))
    x1 = _layer_norm(ALPHA * x_ref[...] + m, lng_ref[...], lnb_ref[...])
    x1_ref[...] = x1

    if with_router:
        x1b_ref[...] = x1.astype(BF16)
        lg = _mm(_split_bf16(x1), (rhi_ref[...], rlo_ref[...]))
        lg = lg.T[0:N_EXPERTS, :]
        eid = lax.broadcasted_iota(jnp.int32, lg.shape, 0)
        neg = jnp.float32(-jnp.inf)
        m1 = jnp.max(lg, axis=0, keepdims=True)
        i1 = jnp.min(jnp.where(lg == m1, eid, N_EXPERTS), axis=0, keepdims=True)
        lg2 = jnp.where(eid == i1, neg, lg)
        m2 = jnp.max(lg2, axis=0, keepdims=True)
        i2 = jnp.min(jnp.where(lg2 == m2, eid, N_EXPERTS), axis=0, keepdims=True)
        e2 = jnp.exp(m2 - m1)
        g1 = 1.0 / (1.0 + e2)
        g2 = e2 / (1.0 + e2)
        rid = lax.broadcasted_iota(jnp.int32, (ROUTE_ROWS, TB), 0)
        slab = jnp.where(rid == ROW_I1, i1.astype(F32), 0.0)
        slab = jnp.where(rid == ROW_I2, i2.astype(F32), slab)
        slab = jnp.where(rid == ROW_G1, g1, slab)
        slab = jnp.where(rid == ROW_G2, g2, slab)
        route_ref[...] = slab


def _mixer_call(x, w, *, n_rows, n_blocks, blocks_per_seq, with_router, sample_state=None, into=None):
    sample = sample_state is not None
    precise = sample
    np_ = 2 if precise else 1
    const2 = lambda i: (0, 0)
    const3 = lambda i: (0, 0, 0)
    const4 = lambda i: (0, 0, 0, 0)

    x_last = x.shape[0] // TB - 1
    in_specs = [pl.BlockSpec((TB, D_MODEL), (lambda i: (x_last, 0)) if sample else (lambda i: (i, 0)))]
    args = [x]
    if not sample:
        in_specs.append(pl.BlockSpec((TB, D_MODEL), lambda i: (jnp.minimum(i + 1, n_blocks - 1), 0)))
        args.append(x)
    if sample:
        conv_in, state_in = sample_state
        in_specs += [pl.BlockSpec(conv_in.shape, const3), pl.BlockSpec(state_in.shape, const4)]
        args += [conv_in, state_in]
    in_specs += [pl.BlockSpec((D_MODEL, IN_COLS_PAD), const2)] * np_
    args += list(w["w_in"][:np_])
    in_specs += [pl.BlockSpec((LANES, GLA_QK), const2)] * np_
    args += list(w["wa2"][:np_])
    in_specs += [pl.BlockSpec((1, GLA_QK), const2), pl.BlockSpec((CONV_W, CONV_DIM), const2),
                 pl.BlockSpec((1, GLA_HEAD_V), const2)]
    args += [w["b_alpha"], w["conv_w"], w["gla_norm_g"]]
    in_specs += [pl.BlockSpec((D_MODEL, D_MODEL), const2)] * np_
    args += list(w["w_out"][:np_])
    in_specs += [pl.BlockSpec((1, D_MODEL), const2)] * 2
    args += [w["ln_g"], w["ln_b"]]
    if with_router:
        in_specs += [pl.BlockSpec((D_MODEL, LANES), const2)] * 2
        args += list(w["router"])
    aliases = {}
    n_alias = 0
    if sample:
        n_alias = len(into)
        for k, buf in enumerate(into):
            aliases[len(args)] = k
            in_specs.append(pl.BlockSpec(memory_space=pl.ANY))
            args.append(buf)

    out_row = (lambda i: (n_blocks, 0)) if sample else (lambda i: (i, 0))
    out_shape = [jax.ShapeDtypeStruct((n_rows, D_MODEL), F32)]
    out_specs = [pl.BlockSpec((TB, D_MODEL), out_row)]
    if with_router:
        out_shape += [jax.ShapeDtypeStruct((n_rows, D_MODEL), BF16),
                      jax.ShapeDtypeStruct((ROUTE_ROWS, n_rows), F32)]
        out_specs += [pl.BlockSpec((TB, D_MODEL), out_row),
                      pl.BlockSpec((ROUTE_ROWS, TB), (lambda i: (0, n_blocks)) if sample else (lambda i: (0, i)))]
    if sample:
        out_shape += [jax.ShapeDtypeStruct(conv_in.shape, F32), jax.ShapeDtypeStruct(state_in.shape, F32)]
        out_specs += [pl.BlockSpec(conv_in.shape, const3), pl.BlockSpec(state_in.shape, const4)]
    else:
        n_seq = n_blocks // blocks_per_seq
        out_shape += [jax.ShapeDtypeStruct((n_seq, CONV_W - 1, CONV_DIM), F32),
                      jax.ShapeDtypeStruct((n_seq, GLA_HEADS, GLA_HEAD_K, GLA_HEAD_V), F32)]
        out_specs += [pl.BlockSpec((1, CONV_W - 1, CONV_DIM), lambda i: (i // blocks_per_seq, 0, 0)),
                      pl.BlockSpec((1, GLA_HEADS, GLA_HEAD_K, GLA_HEAD_V),
                                   lambda i: (i // blocks_per_seq, 0, 0, 0))]
    scratch = [
        pltpu.VMEM((TB, IN_COLS_PAD), F32),
        pltpu.VMEM((TB + 8, CONV_DIM), F32),
        pltpu.VMEM((8, CONV_DIM), F32),
        pltpu.VMEM((CHUNKS_PER_BLOCK, 16, CONV_DIM), F32),
        pltpu.VMEM((TB, D_MODEL), F32),
        pltpu.VMEM((TB, GLA_DV), F32),
    ]
    scratch += [pltpu.VMEM((TB, GLA_QK), BF16)] * np_
    scratch += [pltpu.VMEM((TB, GLA_QK), BF16)] * np_
    scratch += [pltpu.VMEM((TB, GLA_DV), BF16)] * np_
    scratch += [pltpu.VMEM((TB, GLA_QK), BF16)] * 2
    scratch += [pltpu.VMEM((GLA_QK, GLA_HEAD_V), F32)]
    scratch += [pltpu.VMEM((CHUNKS_PER_BLOCK, GLA_HEADS * CHUNK, GLA_QK), BF16)] * np_
    if not sample:
        scratch += [pltpu.VMEM((TB, D_MODEL), BF16)]
    kern = functools.partial(_mixer_kernel, sample=sample, precise=precise, blocks_per_seq=blocks_per_seq,
                             with_router=with_router, n_alias=n_alias)
    name = ("mixer_sample" if sample else "mixer_prompt") + ("_router" if with_router else "")
    return pl.pallas_call(
        kern,
        grid=(1 if sample else n_blocks,),
        in_specs=in_specs,
        out_specs=out_specs,
        out_shape=out_shape,
        scratch_shapes=scratch,
        input_output_aliases=aliases,
        compiler_params=pltpu.CompilerParams(dimension_semantics=("arbitrary",),
                                             vmem_limit_bytes=VMEM_LIMIT_BYTES),
        name=name,
    )(*args)


def _swiglu_step(xp, wg, wu, wd, acc_ref, precise):
    f = pl.program_id(1)
    g = _mm(xp, _parts(wg, precise))
    u = _mm(xp, _parts(wu, precise))
    part = _mm(_parts(_silu(g) * u, precise), _parts(wd, precise))

    @pl.when(f == 0)
    def _():
        acc_ref[...] = part

    @pl.when(f > 0)
    def _():
        acc_ref[...] += part


def _ffn_dense_kernel(*refs, precise, n_alias):
    x_ref, wg_ref, wu_ref, wd_ref, lng_ref, lnb_ref = refs[:6]
    out_ref, acc_ref = refs[6 + n_alias:]
    x = x_ref[...]
    _swiglu_step(_parts(x, precise), wg_ref[0], wu_ref[0], wd_ref[0], acc_ref, precise)

    @pl.when(pl.program_id(1) == pl.num_programs(1) - 1)
    def _():
        out_ref[...] = _layer_norm(ALPHA * x + acc_ref[...], lng_ref[...], lnb_ref[...])


def _ffn_dense_call(x, wg, wu, wd, e, ln_g, ln_b, *, tm, first_block, n_blocks, precise, into=None):
    n_rows = x.shape[0]
    nf = D_FF // TF
    row_spec = pl.BlockSpec((tm, D_MODEL), lambda t, f: (first_block + t, 0))
    in_specs = [row_spec,
                pl.BlockSpec((1, D_MODEL, TF), lambda t, f: (e, 0, f)),
                pl.BlockSpec((1, D_MODEL, TF), lambda t, f: (e, 0, f)),
                pl.BlockSpec((1, TF, D_MODEL), lambda t, f: (e, f, 0)),
                pl.BlockSpec((1, D_MODEL), lambda t, f: (0, 0)),
                pl.BlockSpec((1, D_MODEL), lambda t, f: (0, 0))]
    args = [x, wg, wu, wd, ln_g, ln_b]
    aliases = {}
    n_alias = 0
    if into is not None:
        n_alias = 1
        aliases[len(args)] = 0
        in_specs.append(pl.BlockSpec(memory_space=pl.ANY))
        args.append(into)
    return pl.pallas_call(
        functools.partial(_ffn_dense_kernel, precise=precise, n_alias=n_alias),
        grid=(n_blocks, nf),
        in_specs=in_specs,
        out_specs=row_spec,
        out_shape=jax.ShapeDtypeStruct((n_rows, D_MODEL), F32),
        scratch_shapes=[pltpu.VMEM((tm, D_MODEL), F32)],
        input_output_aliases=aliases,
        compiler_params=pltpu.CompilerParams(dimension_semantics=("arbitrary", "arbitrary"),
                                             vmem_limit_bytes=VMEM_LIMIT_BYTES),
        name="ffn_dense_precise" if precise else "ffn_dense",
    )(*args)


def _ffn_moe_kernel(te_ref, nt_ref, x_ref, wg_ref, wu_ref, wd_ref, *rest, first_tile, n_alias):
    out_ref, acc_ref = rest[n_alias:]
    used = first_tile + pl.program_id(0) < nt_ref[0]

    @pl.when(used)
    def _():
        _swiglu_step((x_ref[...],), wg_ref[0, 0], wu_ref[0, 0], wd_ref[0, 0], acc_ref, False)

    last = pl.program_id(1) == pl.num_programs(1) - 1

    @pl.when(jnp.logical_and(last, used))
    def _():
        out_ref[...] = acc_ref[...]

    @pl.when(jnp.logical_and(last, jnp.logical_not(used)))
    def _():
        out_ref[...] = jnp.zeros_like(out_ref)


def _ffn_moe_call(tile_expert, n_tiles_used, xs, wg, wu, wd, e, *, first_tile, n_sorted, into=None):
    nf = D_FF // TF
    in_specs = [
        pl.BlockSpec((TM_MOE, D_MODEL), lambda t, f, te, nt: (t, 0)),
        pl.BlockSpec((1, 1, D_MODEL, TF), lambda t, f, te, nt: (e, te[first_tile + t], 0, f)),
        pl.BlockSpec((1, 1, D_MODEL, TF), lambda t, f, te, nt: (e, te[first_tile + t], 0, f)),
        pl.BlockSpec((1, 1, TF, D_MODEL), lambda t, f, te, nt: (e, te[first_tile + t], f, 0)),
    ]
    args = [tile_expert, n_tiles_used, xs, wg, wu, wd]
    aliases = {}
    n_alias = 0
    if into is not None:
        n_alias = 1
        aliases[len(args)] = 0
        in_specs.append(pl.BlockSpec(memory_space=pl.ANY))
        args.append(into)
    grid_spec = pltpu.PrefetchScalarGridSpec(
        num_scalar_prefetch=2,
        grid=(xs.shape[0] // TM_MOE, nf),
        in_specs=in_specs,
        out_specs=pl.BlockSpec((TM_MOE, D_MODEL), lambda t, f, te, nt: (first_tile + t, 0)),
        scratch_shapes=[pltpu.VMEM((TM_MOE, D_MODEL), F32)],
    )
    return pl.pallas_call(
        functools.partial(_ffn_moe_kernel, first_tile=first_tile, n_alias=n_alias),
        grid_spec=grid_spec,
        out_shape=jax.ShapeDtypeStruct((n_sorted, D_MODEL), F32),
        input_output_aliases=aliases,
        compiler_params=pltpu.CompilerParams(dimension_semantics=("arbitrary", "arbitrary"),
                                             vmem_limit_bytes=VMEM_LIMIT_BYTES),
        name="ffn_moe",
    )(*args)


def _combine_kernel(x_ref, ya_ref, yb_ref, route_ref, lng_ref, lnb_ref, out_ref):
    pad = jnp.zeros((LANES - ROUTE_ROWS, TB), F32)
    cols = jnp.concatenate([route_ref[...], pad], axis=0).T
    g1 = cols[:, ROW_G1:ROW_G1 + 1]
    g2 = cols[:, ROW_G2:ROW_G2 + 1]
    r = ALPHA * x_ref[...] + (g1 * ya_ref[...] + g2 * yb_ref[...])
    out_ref[...] = _layer_norm(r, lng_ref[...], lnb_ref[...])


def _combine_call(x, ya, yb, route, ln_g, ln_b, *, first_block, n_blocks):
    in_spec = pl.BlockSpec((TB, D_MODEL), lambda t: (first_block + t, 0))
    route_spec = pl.BlockSpec((ROUTE_ROWS, TB), lambda t: (0, first_block + t))
    vec_spec = pl.BlockSpec((1, D_MODEL), lambda t: (0, 0))
    return pl.pallas_call(
        _combine_kernel,
        grid=(n_blocks,),
        in_specs=[in_spec, in_spec, in_spec, route_spec, vec_spec, vec_spec],
        out_specs=pl.BlockSpec((TB, D_MODEL), lambda t: (t, 0)),
        out_shape=jax.ShapeDtypeStruct((n_blocks * TB, D_MODEL), F32),
        compiler_params=pltpu.CompilerParams(dimension_semantics=("arbitrary",),
                                             vmem_limit_bytes=VMEM_LIMIT_BYTES),
        name="moe_combine",
    )(x, ya, yb, route, ln_g, ln_b)


def _take_rows(a, idx):
    return a.at[idx].get(mode="promise_in_bounds")


def _moe_dispatch(route, n_rows):
    i1 = route[ROW_I1].astype(jnp.int32)
    i2 = route[ROW_I2].astype(jnp.int32)
    experts = jnp.arange(N_EXPERTS, dtype=jnp.int32)
    hit = ((i1[:, None] == experts) | (i2[:, None] == experts)).astype(jnp.int32)
    before = jnp.cumsum(hit, axis=0) - hit
    counts = jnp.sum(hit, axis=0)
    padded = ((counts + TM_MOE - 1) // TM_MOE) * TM_MOE
    ends = jnp.cumsum(padded)
    starts = ends - padded
    rank1 = jnp.sum(jnp.where(i1[:, None] == experts, before, 0), axis=1)
    rank2 = jnp.sum(jnp.where(i2[:, None] == experts, before, 0), axis=1)
    p1 = _take_rows(starts, i1) + rank1
    p2 = _take_rows(starts, i2) + rank2
    n_tiles = (2 * n_rows + N_EXPERTS * (TM_MOE - 1) + TM_MOE - 1) // TM_MOE
    tok = jnp.arange(n_rows, dtype=jnp.int32)
    row_src = jnp.zeros((n_tiles * TM_MOE,), jnp.int32).at[jnp.concatenate([p1, p2])].set(
        jnp.concatenate([tok, tok]), mode="promise_in_bounds", unique_indices=True)
    tile_start = jnp.arange(n_tiles, dtype=jnp.int32) * TM_MOE
    tile_expert = jnp.minimum(jnp.sum(tile_start[:, None] >= ends[None, :], axis=1), N_EXPERTS - 1)
    n_tiles_used = (ends[-1] // TM_MOE).reshape(1)
    return row_src, p1, p2, tile_expert.astype(jnp.int32), n_tiles_used.astype(jnp.int32)


def _split_kernel(a_ref, hi_ref, lo_ref):
    hi, lo = _split_bf16(a_ref[...])
    hi_ref[...] = hi
    lo_ref[...] = lo


def _split_call(a):
    rows, cols = a.shape
    tr = min(rows, 256)
    spec = pl.BlockSpec((tr, cols), lambda r: (r, 0))
    return pl.pallas_call(
        _split_kernel,
        grid=(rows // tr,),
        in_specs=[spec],
        out_specs=[spec, spec],
        out_shape=[jax.ShapeDtypeStruct(a.shape, BF16)] * 2,
        name="split_bf16",
    )(a)


def _pad_w_in(w_in):
    main = w_in[:, :COL_A]
    alr = jnp.pad(w_in[:, COL_A:], ((0, 0), (0, LANES - GATE_RANK)))
    return jnp.concatenate([main, alr], axis=1)


def kernel(x_prompt, x_sample, cache_conv, state_gla, ln_mix_g, ln_mix_b, w_in, w_alpha2, b_alpha,
           conv_w, gla_norm_g, w_out, ln_ffn_g, ln_ffn_b, ffn_w_gate, ffn_w_up, ffn_w_down,
           moe_router, moe_w_gate, moe_w_up, moe_w_down):
    n_prompt_seq, seq_len, _ = x_prompt.shape
    n_sample, dec_len, _ = x_sample.shape
    assert dec_len == CHUNK and n_sample * dec_len == TB and seq_len % TB == 0
    n_prompt_rows = n_prompt_seq * seq_len
    n_rows = n_prompt_rows + TB
    blocks_per_seq = seq_len // TB
    n_prompt_blocks = n_prompt_rows // TB
    assert n_prompt_rows % TM_DENSE == 0

    xp = x_prompt.reshape(n_prompt_rows, D_MODEL)
    xs = x_sample.reshape(TB, D_MODEL)
    gla_p, conv_p, gla_s, conv_s = [], [], [], []
    for l in range(DEPTH):
        moe_layer = l % 2 == 1
        e = l // 2
        w = {
            "w_in": _split_call(_pad_w_in(w_in[l])),
            "wa2": _split_call(jnp.pad(w_alpha2[l], ((0, LANES - GATE_RANK), (0, 0)))),
            "b_alpha": b_alpha[l][None, :],
            "conv_w": conv_w[l],
            "gla_norm_g": gla_norm_g[l][None, :],
            "w_out": _split_call(w_out[l]),
            "ln_g": ln_mix_g[l][None, :],
            "ln_b": ln_mix_b[l][None, :],
        }
        if moe_layer:
            w["router"] = _split_call(jnp.pad(moe_router[e], ((0, 0), (0, LANES - N_EXPERTS))))
        common = dict(n_rows=n_rows, n_blocks=n_prompt_blocks, blocks_per_seq=blocks_per_seq,
                      with_router=moe_layer)
        outs = _mixer_call(xp, w, **common)
        n_stream = 3 if moe_layer else 1
        cp, sp = outs[n_stream:]
        outs_s = _mixer_call(xs, w, sample_state=(cache_conv[l], state_gla[l]), into=outs[:n_stream], **common)
        cs, ss = outs_s[n_stream:]
        conv_p.append(cp)
        gla_p.append(sp)
        conv_s.append(cs)
        gla_s.append(ss)

        ln_g = ln_ffn_g[l][None, :]
        ln_b = ln_ffn_b[l][None, :]
        if moe_layer:
            x1, x1b, route = outs_s[:3]
            row_src, p1, p2, tile_expert, n_tiles_used = _moe_dispatch(route, n_rows)
            n_tiles = row_src.shape[0] // TM_MOE
            bounds = [n_tiles * c // MOE_CALLS for c in range(MOE_CALLS + 1)]
            ys = None
            for t0, t1 in zip(bounds[:-1], bounds[1:]):
                xsort = _take_rows(x1b, row_src[t0 * TM_MOE:t1 * TM_MOE])
                ys = _ffn_moe_call(tile_expert, n_tiles_used, xsort, moe_w_gate, moe_w_up, moe_w_down, e,
                                   first_tile=t0, n_sorted=row_src.shape[0], into=ys)
            comb = functools.partial(_combine_call, x1, _take_rows(ys, p1), _take_rows(ys, p2), route, ln_g, ln_b)
            if l == DEPTH - 1:
                xp = comb(first_block=0, n_blocks=n_prompt_blocks)
                xs = comb(first_block=n_prompt_blocks, n_blocks=1)
            else:
                xp = xs = comb(first_block=0, n_blocks=n_prompt_blocks + 1)
        else:
            x1 = outs_s[0]
            dense = functools.partial(_ffn_dense_call, x1, ffn_w_gate, ffn_w_up, ffn_w_down, e, ln_g, ln_b)
            x2 = dense(tm=TM_DENSE, first_block=0, n_blocks=n_prompt_rows // TM_DENSE, precise=False)
            xp = xs = dense(tm=TB, first_block=n_prompt_blocks, n_blocks=1, precise=(l == 0), into=x2)

    y_prompt = xp[:n_prompt_rows].reshape(n_prompt_seq, seq_len, D_MODEL)
    y_sample = xs[xs.shape[0] - TB:].reshape(n_sample, dec_len, D_MODEL)
    return (y_prompt, y_sample, jnp.stack(gla_p), jnp.stack(conv_p), jnp.stack(gla_s), jnp.stack(conv_s))
```

```python
import functools

import jax
import jax.numpy as jnp
from jax import lax
from jax.experimental import pallas as pl
from jax.experimental.pallas import tpu as pltpu

F32 = jnp.float32
BF16 = jnp.bfloat16

D_MODEL = 1024
DEPTH = 4
CHUNK = 64
CONV_DIM = 512
CONV_W = 3
GLA_HEADS = 4
GLA_HEAD_K = 64
GLA_HEAD_V = 128
GLA_QK = GLA_HEADS * GLA_HEAD_K
GLA_DV = GLA_HEADS * GLA_HEAD_V
GATE_RANK = 16
GATE_TEMP = 16.0
D_FF = 3584
N_EXPERTS = 8
ALPHA = (2 * DEPTH) ** 0.25
LN_EPS = 1e-5

LANES = 128
SUBLANES = 8
VMEM_LIMIT_BYTES = 56 * 1024 * 1024

COL_CB, COL_CC, COL_CH = 0, CONV_DIM, 2 * CONV_DIM
COL_Q = 3 * CONV_DIM
COL_K = COL_Q + GLA_QK
COL_V = COL_K + GLA_QK
COL_G = COL_V + GLA_DV
COL_A = COL_G + GLA_DV
IN_COLS_PAD = COL_A + LANES

TB = 512
SUB = 256
CHUNKS_PER_BLOCK = TB // CHUNK
TF = 512
TM_MOE = 1024
TM_DENSE = 1024
MOE_CALLS = 4
ROUTE_ROWS = 8
ROW_I1, ROW_I2, ROW_G1, ROW_G2 = 0, 1, 2, 3


def _dot(a, b):
    return jnp.dot(a, b, preferred_element_type=F32)


def _dot_tn(a, b):
    return lax.dot_general(a, b, (((0,), (0,)), ((), ())), preferred_element_type=F32)


def _dot_nt(a, b):
    return lax.dot_general(a, b, (((1,), (1,)), ((), ())), preferred_element_type=F32)


def _split_bf16(a):
    hi = a.astype(BF16)
    lo = (a - hi.astype(F32)).astype(BF16)
    return hi, lo


def _parts(a, precise):
    return _split_bf16(a) if precise else (a.astype(BF16),)


def _mm(a, b, dot=_dot):
    r = dot(a[0], b[0])
    if len(a) > 1:
        r = r + dot(a[1], b[0])
    if len(b) > 1:
        r = r + dot(a[0], b[1])
    return r


def _load(refs, idx=None):
    return tuple(r[...] if idx is None else r[idx] for r in refs)


def _store(refs, idx, parts):
    for r, p in zip(refs, parts):
        r[idx] = p


def _layer_norm(r, g, b):
    mu = jnp.mean(r, axis=-1, keepdims=True)
    d = r - mu
    var = jnp.mean(d * d, axis=-1, keepdims=True)
    return d * lax.rsqrt(var + LN_EPS) * g + b


def _silu(g):
    return g / (1.0 + jnp.exp(-g))


def _mixer_kernel(*refs, sample, precise, blocks_per_seq, with_router, n_alias):
    np_ = 2 if precise else 1
    it = iter(refs)

    def take(n):
        return [next(it) for _ in range(n)]

    x_ref = next(it)
    if not sample:
        xn_ref = next(it)
    if sample:
        convin_ref = next(it)
        statein_ref = next(it)
    w_in_refs = take(np_)
    wa2_refs = take(np_)
    balpha_ref = next(it)
    convw_ref = next(it)
    gnorm_ref = next(it)
    w_out_refs = take(np_)
    lng_ref = next(it)
    lnb_ref = next(it)
    if with_router:
        rhi_ref = next(it)
        rlo_ref = next(it)
    take(n_alias)
    x1_ref = next(it)
    if with_router:
        x1b_ref = next(it)
        route_ref = next(it)
    convout_ref = next(it)
    stateout_ref = next(it)
    proj_ref = next(it)
    ue_ref = next(it)
    prev_ref = next(it)
    fx_ref = next(it)
    y_ref = next(it)
    o_ref = next(it)
    qg_refs = take(np_)
    kd_refs = take(np_)
    v_refs = take(np_)
    la_refs = take(2)
    s_ref = next(it)
    qms_refs = take(np_)
    if not sample:
        xnb_ref = next(it)

    i = pl.program_id(0)
    seq_start = (i % blocks_per_seq) == 0

    def in_proj(src, cols):
        proj_ref[:, cols] = _mm(src, tuple(r[:, cols] for r in w_in_refs))

    if sample:
        in_proj(_parts(x_ref[...], precise), slice(None))
        project_ahead = lambda cols: None
    else:
        @pl.when(i == 0)
        def _():
            in_proj(_parts(x_ref[...], precise), slice(None))

        xnb_ref[...] = xn_ref[...].astype(BF16)
        project_ahead = lambda cols: in_proj((xnb_ref[...],), cols)

    if sample:
        prev_ref[6:8, :] = convin_ref[0]
    else:
        @pl.when(seq_start)
        def _():
            prev_ref[...] = jnp.zeros_like(prev_ref)

    w0 = convw_ref[0:1, :]
    w1 = convw_ref[1:2, :]
    w2 = convw_ref[2:3, :]
    u = proj_ref[:, COL_CC:COL_CC + CONV_DIM] * proj_ref[:, COL_CH:COL_CH + CONV_DIM]
    ue_ref[0:8, :] = prev_ref[...]
    ue_ref[8:8 + TB, :] = u
    conv = ue_ref[6:6 + TB, :] * w0 + ue_ref[7:7 + TB, :] * w1 + u * w2
    y_ref[:, 0:CONV_DIM] = proj_ref[:, COL_CB:COL_CB + CONV_DIM] * conv
    if sample:
        for c in range(CHUNKS_PER_BLOCK):
            r0 = c * CHUNK
            if c > 0:
                fx_ref[c, 6:8, :] = convin_ref[c]
                fx_ref[c, 8:16, :] = ue_ref[8 + r0:16 + r0, :]
                convc = fx_ref[c, 6:14, :] * w0 + fx_ref[c, 7:15, :] * w1 + fx_ref[c, 8:16, :] * w2
                y_ref[r0:r0 + 8, 0:CONV_DIM] = proj_ref[r0:r0 + 8, COL_CB:COL_CB + CONV_DIM] * convc
            convout_ref[c] = ue_ref[8 + r0 + CHUNK - 2:8 + r0 + CHUNK, :]
    else:
        prev_ref[...] = ue_ref[TB:TB + 8, :]
        convout_ref[0] = ue_ref[TB + 6:TB + 8, :]
    project_ahead(slice(COL_CB, COL_Q))

    z = _mm(_parts(proj_ref[:, COL_A:COL_A + LANES], precise), _load(wa2_refs)) + balpha_ref[...]
    la = (jnp.minimum(z, 0.0) - jnp.log(1.0 + jnp.exp(-jnp.abs(z)))) * (1.0 / GATE_TEMP)
    _store(la_refs, slice(None), _split_bf16(la))

    row = lax.broadcasted_iota(jnp.int32, (SUB, SUB), 0)
    col = lax.broadcasted_iota(jnp.int32, (SUB, SUB), 1)
    same_chunk = (row // CHUNK) == (col // CHUNK)
    causal = jnp.logical_and(same_chunk, row >= col)
    tril_bd = (causal.astype(BF16),)
    ones_bd = (same_chunk.astype(BF16),)
    lane_head = lax.broadcasted_iota(jnp.int32, (SUB, GLA_QK), 1) // GLA_HEAD_K

    for sb in range(TB // SUB):
        rs = slice(sb * SUB, (sb + 1) * SUB)
        lap = _load(la_refs, (rs, slice(None)))
        bcum = _mm(tril_bd, lap)
        blast = _mm(ones_bd, lap)
        q = proj_ref[rs, COL_Q:COL_Q + GLA_QK]
        k = proj_ref[rs, COL_K:COL_K + GLA_QK]
        qg = q * jnp.exp(bcum) * (GLA_HEAD_K ** -0.5)
        kg = _parts(k * jnp.exp(-bcum), precise)
        _store(qg_refs, (rs, slice(None)), _parts(qg, precise))
        _store(kd_refs, (rs, slice(None)), _parts(k * jnp.exp(blast - bcum), precise))
        vp = _parts(proj_ref[rs, COL_V:COL_V + GLA_DV], precise)
        _store(v_refs, (rs, slice(None)), vp)
        for h in range(GLA_HEADS):
            vs = slice(h * GLA_HEAD_V, (h + 1) * GLA_HEAD_V)
            qm = _parts(jnp.where(lane_head == h, qg, 0.0), precise)
            for cc in range(SUB // CHUNK):
                dst = (sb * (SUB // CHUNK) + cc, slice(h * CHUNK, (h + 1) * CHUNK), slice(None))
                _store(qms_refs, dst, tuple(p[cc * CHUNK:(cc + 1) * CHUNK] for p in qm))
            sc = jnp.where(causal, _mm(qm, kg, _dot_nt), 0.0)
            o_ref[rs, vs] = _mm(_parts(sc, precise), tuple(p[:, vs] for p in vp))

    project_ahead(slice(COL_Q, COL_G))
    project_ahead(slice(COL_A, IN_COLS_PAD))
    trow = lax.broadcasted_iota(jnp.int32, (TB, LANES), 0) // CHUNK
    tcol = lax.broadcasted_iota(jnp.int32, (TB, LANES), 1)
    chunk_sel = ((trow == tcol).astype(BF16),)
    dec_all = jnp.exp(_mm(_load(la_refs), chunk_sel, _dot_tn))
    if not sample:
        s = jnp.where(seq_start, 0.0, s_ref[...])
    for c in range(CHUNKS_PER_BLOCK):
        cs = slice(c * CHUNK, (c + 1) * CHUNK)
        if sample:
            s = statein_ref[c].reshape(GLA_QK, GLA_HEAD_V)
        oi = _mm(_load(qms_refs, c), _parts(s, precise))
        for h in range(GLA_HEADS):
            o_ref[cs, h * GLA_HEAD_V:(h + 1) * GLA_HEAD_V] += oi[h * CHUNK:(h + 1) * CHUNK, :]
        ds = _mm(_load(kd_refs, (cs, slice(None))), _load(v_refs, (cs, slice(None))), _dot_tn)
        ds_diag = jnp.concatenate([ds[h * GLA_HEAD_K:(h + 1) * GLA_HEAD_K, h * GLA_HEAD_V:(h + 1) * GLA_HEAD_V]
                                   for h in range(GLA_HEADS)], axis=0)
        s = dec_all[:, c:c + 1] * s + ds_diag
        if sample:
            stateout_ref[c] = s.reshape(GLA_HEADS, GLA_HEAD_K, GLA_HEAD_V)
    if not sample:
        s_ref[...] = s
        stateout_ref[0] = s.reshape(GLA_HEADS, GLA_HEAD_K, GLA_HEAD_V)

    gn = gnorm_ref[...]
    for h in range(GLA_HEADS):
        vs = slice(h * GLA_HEAD_V, (h + 1) * GLA_HEAD_V)
        oh = o_ref[:, vs]
        ms = jnp.mean(oh * oh, axis=-1, keepdims=True)
        g = proj_ref[:, COL_G + h * GLA_HEAD_V:COL_G + (h + 1) * GLA_HEAD_V]
        y_ref[:, CONV_DIM + h * GLA_HEAD_V:CONV_DIM + (h + 1) * GLA_HEAD_V] = (
            oh * lax.rsqrt(ms + LN_EPS) * gn * _silu(g))

    project_ahead(slice(COL_G, COL_A))

    m = _mm(_parts(y_ref[...], precise), _load(w_out_refs))
    x1 = _layer_norm(ALPHA * x_ref[...] + m, lng_ref[...], lnb_ref[...])
    x1_ref[...] = x1

    if with_router:
        x1b_ref[...] = x1.astype(BF16)
        lg = _mm(_split_bf16(x1), (rhi_ref[...], rlo_ref[...]))
        lg = lg.T[0:N_EXPERTS, :]
        eid = lax.broadcasted_iota(jnp.int32, lg.shape, 0)
        neg = jnp.float32(-jnp.inf)
        m1 = jnp.max(lg, axis=0, keepdims=True)
        i1 = jnp.min(jnp.where(lg == m1, eid, N_EXPERTS), axis=0, keepdims=True)
        lg2 = jnp.where(eid == i1, neg, lg)
        m2 = jnp.max(lg2, axis=0, keepdims=True)
        i2 = jnp.min(jnp.where(lg2 == m2, eid, N_EXPERTS), axis=0, keepdims=True)
        e2 = jnp.exp(m2 - m1)
        g1 = 1.0 / (1.0 + e2)
        g2 = e2 / (1.0 + e2)
        rid = lax.broadcasted_iota(jnp.int32, (ROUTE_ROWS, TB), 0)
        slab = jnp.where(rid == ROW_I1, i1.astype(F32), 0.0)
        slab = jnp.where(rid == ROW_I2, i2.astype(F32), slab)
        slab = jnp.where(rid == ROW_G1, g1, slab)
        slab = jnp.where(rid == ROW_G2, g2, slab)
        route_ref[...] = slab


def _mixer_call(x, w, *, n_rows, n_blocks, blocks_per_seq, with_router, sample_state=None, into=None):
    sample = sample_state is not None
    precise = sample
    np_ = 2 if precise else 1
    const2 = lambda i: (0, 0)
    const3 = lambda i: (0, 0, 0)
    const4 = lambda i: (0, 0, 0, 0)

    x_last = x.shape[0] // TB - 1
    in_specs = [pl.BlockSpec((TB, D_MODEL), (lambda i: (x_last, 0)) if sample else (lambda i: (i, 0)))]
    args = [x]
    if not sample:
        in_specs.append(pl.BlockSpec((TB, D_MODEL), lambda i: (jnp.minimum(i + 1, n_blocks - 1), 0)))
        args.append(x)
    if sample:
        conv_in, state_in = sample_state
        in_specs += [pl.BlockSpec(conv_in.shape, const3), pl.BlockSpec(state_in.shape, const4)]
        args += [conv_in, state_in]
    in_specs += [pl.BlockSpec((D_MODEL, IN_COLS_PAD), const2)] * np_
    args += list(w["w_in"][:np_])
    in_specs += [pl.BlockSpec((LANES, GLA_QK), const2)] * np_
    args += list(w["wa2"][:np_])
    in_specs += [pl.BlockSpec((1, GLA_QK), const2), pl.BlockSpec((CONV_W, CONV_DIM), const2),
                 pl.BlockSpec((1, GLA_HEAD_V), const2)]
    args += [w["b_alpha"], w["conv_w"], w["gla_norm_g"]]
    in_specs += [pl.BlockSpec((D_MODEL, D_MODEL), const2)] * np_
    args += list(w["w_out"][:np_])
    in_specs += [pl.BlockSpec((1, D_MODEL), const2)] * 2
    args += [w["ln_g"], w["ln_b"]]
    if with_router:
        in_specs += [pl.BlockSpec((D_MODEL, LANES), const2)] * 2
        args += list(w["router"])
    aliases = {}
    n_alias = 0
    if sample:
        n_alias = len(into)
        for k, buf in enumerate(into):
            aliases[len(args)] = k
            in_specs.append(pl.BlockSpec(memory_space=pl.ANY))
            args.append(buf)

    out_row = (lambda i: (n_blocks, 0)) if sample else (lambda i: (i, 0))
    out_shape = [jax.ShapeDtypeStruct((n_rows, D_MODEL), F32)]
    out_specs = [pl.BlockSpec((TB, D_MODEL), out_row)]
    if with_router:
        out_shape += [jax.ShapeDtypeStruct((n_rows, D_MODEL), BF16),
                      jax.ShapeDtypeStruct((ROUTE_ROWS, n_rows), F32)]
        out_specs += [pl.BlockSpec((TB, D_MODEL), out_row),
                      pl.BlockSpec((ROUTE_ROWS, TB), (lambda i: (0, n_blocks)) if sample else (lambda i: (0, i)))]
    if sample:
        out_shape += [jax.ShapeDtypeStruct(conv_in.shape, F32), jax.ShapeDtypeStruct(state_in.shape, F32)]
        out_specs += [pl.BlockSpec(conv_in.shape, const3), pl.BlockSpec(state_in.shape, const4)]
    else:
        n_seq = n_blocks // blocks_per_seq
        out_shape += [jax.ShapeDtypeStruct((n_seq, CONV_W - 1, CONV_DIM), F32),
                      jax.ShapeDtypeStruct((n_seq, GLA_HEADS, GLA_HEAD_K, GLA_HEAD_V), F32)]
        out_specs += [pl.BlockSpec((1, CONV_W - 1, CONV_DIM), lambda i: (i // blocks_per_seq, 0, 0)),
                      pl.BlockSpec((1, GLA_HEADS, GLA_HEAD_K, GLA_HEAD_V),
                                   lambda i: (i // blocks_per_seq, 0, 0, 0))]
    scratch = [
        pltpu.VMEM((TB, IN_COLS_PAD), F32),
        pltpu.VMEM((TB + 8, CONV_DIM), F32),
        pltpu.VMEM((8, CONV_DIM), F32),
        pltpu.VMEM((CHUNKS_PER_BLOCK, 16, CONV_DIM), F32),
        pltpu.VMEM((TB, D_MODEL), F32),
        pltpu.VMEM((TB, GLA_DV), F32),
    ]
    scratch += [pltpu.VMEM((TB, GLA_QK), BF16)] * np_
    scratch += [pltpu.VMEM((TB, GLA_QK), BF16)] * np_
    scratch += [pltpu.VMEM((TB, GLA_DV), BF16)] * np_
    scratch += [pltpu.VMEM((TB, GLA_QK), BF16)] * 2
    scratch += [pltpu.VMEM((GLA_QK, GLA_HEAD_V), F32)]
    scratch += [pltpu.VMEM((CHUNKS_PER_BLOCK, GLA_HEADS * CHUNK, GLA_QK), BF16)] * np_
    if not sample:
        scratch += [pltpu.VMEM((TB, D_MODEL), BF16)]
    kern = functools.partial(_mixer_kernel, sample=sample, precise=precise, blocks_per_seq=blocks_per_seq,
                             with_router=with_router, n_alias=n_alias)
    name = ("mixer_sample" if sample else "mixer_prompt") + ("_router" if with_router else "")
    return pl.pallas_call(
        kern,
        grid=(1 if sample else n_blocks,),
        in_specs=in_specs,
        out_specs=out_specs,
        out_shape=out_shape,
        scratch_shapes=scratch,
        input_output_aliases=aliases,
        compiler_params=pltpu.CompilerParams(dimension_semantics=("arbitrary",),
                                             vmem_limit_bytes=VMEM_LIMIT_BYTES),
        name=name,
    )(*args)


def _swiglu_step(xp, wg, wu, wd, acc_ref, precise):
    f = pl.program_id(1)
    g = _mm(xp, _parts(wg, precise))
    u = _mm(xp, _parts(wu, precise))
    part = _mm(_parts(_silu(g) * u, precise), _parts(wd, precise))

    @pl.when(f == 0)
    def _():
        acc_ref[...] = part

    @pl.when(f > 0)
    def _():
        acc_ref[...] += part


def _ffn_dense_kernel(*refs, precise, n_alias):
    x_ref, wg_ref, wu_ref, wd_ref, lng_ref, lnb_ref = refs[:6]
    out_ref, acc_ref = refs[6 + n_alias:]
    x = x_ref[...]
    _swiglu_step(_parts(x, precise), wg_ref[0], wu_ref[0], wd_ref[0], acc_ref, precise)

    @pl.when(pl.program_id(1) == pl.num_programs(1) - 1)
    def _():
        out_ref[...] = _layer_norm(ALPHA * x + acc_ref[...], lng_ref[...], lnb_ref[...])


def _ffn_dense_call(x, wg, wu, wd, e, ln_g, ln_b, *, tm, first_block, n_blocks, precise, into=None):
    n_rows = x.shape[0]
    nf = D_FF // TF
    row_spec = pl.BlockSpec((tm, D_MODEL), lambda t, f: (first_block + t, 0))
    in_specs = [row_spec,
                pl.BlockSpec((1, D_MODEL, TF), lambda t, f: (e, 0, f)),
                pl.BlockSpec((1, D_MODEL, TF), lambda t, f: (e, 0, f)),
                pl.BlockSpec((1, TF, D_MODEL), lambda t, f: (e, f, 0)),
                pl.BlockSpec((1, D_MODEL), lambda t, f: (0, 0)),
                pl.BlockSpec((1, D_MODEL), lambda t, f: (0, 0))]
    args = [x, wg, wu, wd, ln_g, ln_b]
    aliases = {}
    n_alias = 0
    if into is not None:
        n_alias = 1
        aliases[len(args)] = 0
        in_specs.append(pl.BlockSpec(memory_space=pl.ANY))
        args.append(into)
    return pl.pallas_call(
        functools.partial(_ffn_dense_kernel, precise=precise, n_alias=n_alias),
        grid=(n_blocks, nf),
        in_specs=in_specs,
        out_specs=row_spec,
        out_shape=jax.ShapeDtypeStruct((n_rows, D_MODEL), F32),
        scratch_shapes=[pltpu.VMEM((tm, D_MODEL), F32)],
        input_output_aliases=aliases,
        compiler_params=pltpu.CompilerParams(dimension_semantics=("arbitrary", "arbitrary"),
                                             vmem_limit_bytes=VMEM_LIMIT_BYTES),
        name="ffn_dense_precise" if precise else "ffn_dense",
    )(*args)


def _ffn_moe_kernel(te_ref, nt_ref, x_ref, wg_ref, wu_ref, wd_ref, *rest, first_tile, n_alias):
    out_ref, acc_ref = rest[n_alias:]
    used = first_tile + pl.program_id(0) < nt_ref[0]

    @pl.when(used)
    def _():
        _swiglu_step((x_ref[...],), wg_ref[0, 0], wu_ref[0, 0], wd_ref[0, 0], acc_ref, False)

    last = pl.program_id(1) == pl.num_programs(1) - 1

    @pl.when(jnp.logical_and(last, used))
    def _():
        out_ref[...] = acc_ref[...]

    @pl.when(jnp.logical_and(last, jnp.logical_not(used)))
    def _():
        out_ref[...] = jnp.zeros_like(out_ref)


def _ffn_moe_call(tile_expert, n_tiles_used, xs, wg, wu, wd, e, *, first_tile, n_sorted, into=None):
    nf = D_FF // TF
    in_specs = [
        pl.BlockSpec((TM_MOE, D_MODEL), lambda t, f, te, nt: (t, 0)),
        pl.BlockSpec((1, 1, D_MODEL, TF), lambda t, f, te, nt: (e, te[first_tile + t], 0, f)),
        pl.BlockSpec((1, 1, D_MODEL, TF), lambda t, f, te, nt: (e, te[first_tile + t], 0, f)),
        pl.BlockSpec((1, 1, TF, D_MODEL), lambda t, f, te, nt: (e, te[first_tile + t], f, 0)),
    ]
    args = [tile_expert, n_tiles_used, xs, wg, wu, wd]
    aliases = {}
    n_alias = 0
    if into is not None:
        n_alias = 1
        aliases[len(args)] = 0
        in_specs.append(pl.BlockSpec(memory_space=pl.ANY))
        args.append(into)
    grid_spec = pltpu.PrefetchScalarGridSpec(
        num_scalar_prefetch=2,
        grid=(xs.shape[0] // TM_MOE, nf),
        in_specs=in_specs,
        out_specs=pl.BlockSpec((TM_MOE, D_MODEL), lambda t, f, te, nt: (first_tile + t, 0)),
        scratch_shapes=[pltpu.VMEM((TM_MOE, D_MODEL), F32)],
    )
    return pl.pallas_call(
        functools.partial(_ffn_moe_kernel, first_tile=first_tile, n_alias=n_alias),
        grid_spec=grid_spec,
        out_shape=jax.ShapeDtypeStruct((n_sorted, D_MODEL), F32),
        input_output_aliases=aliases,
        compiler_params=pltpu.CompilerParams(dimension_semantics=("arbitrary", "arbitrary"),
                                             vmem_limit_bytes=VMEM_LIMIT_BYTES),
        name="ffn_moe",
    )(*args)


def _combine_kernel(x_ref, ya_ref, yb_ref, route_ref, lng_ref, lnb_ref, out_ref):
    pad = jnp.zeros((LANES - ROUTE_ROWS, TB), F32)
    cols = jnp.concatenate([route_ref[...], pad], axis=0).T
    g1 = cols[:, ROW_G1:ROW_G1 + 1]
    g2 = cols[:, ROW_G2:ROW_G2 + 1]
    r = ALPHA * x_ref[...] + (g1 * ya_ref[...] + g2 * yb_ref[...])
    out_ref[...] = _layer_norm(r, lng_ref[...], lnb_ref[...])


def _combine_call(x, ya, yb, route, ln_g, ln_b, *, first_block, n_blocks):
    in_spec = pl.BlockSpec((TB, D_MODEL), lambda t: (first_block + t, 0))
    route_spec = pl.BlockSpec((ROUTE_ROWS, TB), lambda t: (0, first_block + t))
    vec_spec = pl.BlockSpec((1, D_MODEL), lambda t: (0, 0))
    return pl.pallas_call(
        _combine_kernel,
        grid=(n_blocks,),
        in_specs=[in_spec, in_spec, in_spec, route_spec, vec_spec, vec_spec],
        out_specs=pl.BlockSpec((TB, D_MODEL), lambda t: (t, 0)),
        out_shape=jax.ShapeDtypeStruct((n_blocks * TB, D_MODEL), F32),
        compiler_params=pltpu.CompilerParams(dimension_semantics=("arbitrary",),
                                             vmem_limit_bytes=VMEM_LIMIT_BYTES),
        name="moe_combine",
    )(x, ya, yb, route, ln_g, ln_b)


def _take_rows(a, idx):
    return a.at[idx].get(mode="promise_in_bounds")


def _moe_dispatch(route, n_rows):
    i1 = route[ROW_I1].astype(jnp.int32)
    i2 = route[ROW_I2].astype(jnp.int32)
    experts = jnp.arange(N_EXPERTS, dtype=jnp.int32)
    hit = ((i1[:, None] == experts) | (i2[:, None] == experts)).astype(jnp.int32)
    before = jnp.cumsum(hit, axis=0) - hit
    counts = jnp.sum(hit, axis=0)
    padded = ((counts + TM_MOE - 1) // TM_MOE) * TM_MOE
    ends = jnp.cumsum(padded)
    starts = ends - padded
    rank1 = jnp.sum(jnp.where(i1[:, None] == experts, before, 0), axis=1)
    rank2 = jnp.sum(jnp.where(i2[:, None] == experts, before, 0), axis=1)
    p1 = _take_rows(starts, i1) + rank1
    p2 = _take_rows(starts, i2) + rank2
    n_tiles = (2 * n_rows + N_EXPERTS * (TM_MOE - 1) + TM_MOE - 1) // TM_MOE
    tile_start = jnp.arange(n_tiles, dtype=jnp.int32) * TM_MOE
    tile_expert = jnp.minimum(jnp.sum(tile_start[:, None] >= ends[None, :], axis=1), N_EXPERTS - 1)
    tile_expert = tile_expert.astype(jnp.int32)
    n_tiles_used = (ends[-1] // TM_MOE).reshape(1)
    tok = jnp.arange(n_rows, dtype=jnp.int32)
    _, by_pos = lax.sort((jnp.concatenate([p1, p2]), jnp.concatenate([tok, tok])), num_keys=1)
    pad_before = jnp.repeat(_take_rows(starts - (jnp.cumsum(counts) - counts), tile_expert), TM_MOE)
    pos = jnp.arange(n_tiles * TM_MOE, dtype=jnp.int32)
    row_src = _take_rows(by_pos, jnp.clip(pos - pad_before, 0, 2 * n_rows - 1))
    return row_src, p1, p2, tile_expert, n_tiles_used.astype(jnp.int32)


def _split_kernel(a_ref, hi_ref, lo_ref):
    hi, lo = _split_bf16(a_ref[...])
    hi_ref[...] = hi
    lo_ref[...] = lo


def _split_call(a):
    rows, cols = a.shape
    tr = min(rows, 256)
    spec = pl.BlockSpec((tr, cols), lambda r: (r, 0))
    return pl.pallas_call(
        _split_kernel,
        grid=(rows // tr,),
        in_specs=[spec],
        out_specs=[spec, spec],
        out_shape=[jax.ShapeDtypeStruct(a.shape, BF16)] * 2,
        name="split_bf16",
    )(a)


def _pad_w_in(w_in):
    main = w_in[:, :COL_A]
    alr = jnp.pad(w_in[:, COL_A:], ((0, 0), (0, LANES - GATE_RANK)))
    return jnp.concatenate([main, alr], axis=1)


def kernel(x_prompt, x_sample, cache_conv, state_gla, ln_mix_g, ln_mix_b, w_in, w_alpha2, b_alpha,
           conv_w, gla_norm_g, w_out, ln_ffn_g, ln_ffn_b, ffn_w_gate, ffn_w_up, ffn_w_down,
           moe_router, moe_w_gate, moe_w_up, moe_w_down):
    n_prompt_seq, seq_len, _ = x_prompt.shape
    n_sample, dec_len, _ = x_sample.shape
    assert dec_len == CHUNK and n_sample * dec_len == TB and seq_len % TB == 0
    n_prompt_rows = n_prompt_seq * seq_len
    n_rows = n_prompt_rows + TB
    blocks_per_seq = seq_len // TB
    n_prompt_blocks = n_prompt_rows // TB
    assert n_prompt_rows % TM_DENSE == 0

    xp = x_prompt.reshape(n_prompt_rows, D_MODEL)
    xs = x_sample.reshape(TB, D_MODEL)
    gla_p, conv_p, gla_s, conv_s = [], [], [], []
    for l in range(DEPTH):
        moe_layer = l % 2 == 1
        e = l // 2
        w = {
            "w_in": _split_call(_pad_w_in(w_in[l])),
            "wa2": _split_call(jnp.pad(w_alpha2[l], ((0, LANES - GATE_RANK), (0, 0)))),
            "b_alpha": b_alpha[l][None, :],
            "conv_w": conv_w[l],
            "gla_norm_g": gla_norm_g[l][None, :],
            "w_out": _split_call(w_out[l]),
            "ln_g": ln_mix_g[l][None, :],
            "ln_b": ln_mix_b[l][None, :],
        }
        if moe_layer:
            w["router"] = _split_call(jnp.pad(moe_router[e], ((0, 0), (0, LANES - N_EXPERTS))))
        common = dict(n_rows=n_rows, n_blocks=n_prompt_blocks, blocks_per_seq=blocks_per_seq,
                      with_router=moe_layer)
        outs = _mixer_call(xp, w, **common)
        n_stream = 3 if moe_layer else 1
        cp, sp = outs[n_stream:]
        outs_s = _mixer_call(xs, w, sample_state=(cache_conv[l], state_gla[l]), into=outs[:n_stream], **common)
        cs, ss = outs_s[n_stream:]
        conv_p.append(cp)
        gla_p.append(sp)
        conv_s.append(cs)
        gla_s.append(ss)

        ln_g = ln_ffn_g[l][None, :]
        ln_b = ln_ffn_b[l][None, :]
        if moe_layer:
            x1, x1b, route = outs_s[:3]
            row_src, p1, p2, tile_expert, n_tiles_used = _moe_dispatch(route, n_rows)
            n_tiles = row_src.shape[0] // TM_MOE
            bounds = [n_tiles * c // MOE_CALLS for c in range(MOE_CALLS + 1)]
            ys = None
            for t0, t1 in zip(bounds[:-1], bounds[1:]):
                xsort = _take_rows(x1b, row_src[t0 * TM_MOE:t1 * TM_MOE])
                ys = _ffn_moe_call(tile_expert, n_tiles_used, xsort, moe_w_gate, moe_w_up, moe_w_down, e,
                                   first_tile=t0, n_sorted=row_src.shape[0], into=ys)
            comb = functools.partial(_combine_call, x1, _take_rows(ys, p1), _take_rows(ys, p2), route, ln_g, ln_b)
            if l == DEPTH - 1:
                xp = comb(first_block=0, n_blocks=n_prompt_blocks)
                xs = comb(first_block=n_prompt_blocks, n_blocks=1)
            else:
                xp = xs = comb(first_block=0, n_blocks=n_prompt_blocks + 1)
        else:
            x1 = outs_s[0]
            dense = functools.partial(_ffn_dense_call, x1, ffn_w_gate, ffn_w_up, ffn_w_down, e, ln_g, ln_b)
            x2 = dense(tm=TM_DENSE, first_block=0, n_blocks=n_prompt_rows // TM_DENSE, precise=False)
            xp = xs = dense(tm=TB, first_block=n_prompt_blocks, n_blocks=1, precise=(l == 0), into=x2)

    y_prompt = xp[:n_prompt_rows].reshape(n_prompt_seq, seq_len, D_MODEL)
    y_sample = xs[xs.shape[0] - TB:].reshape(n_sample, dec_len, D_MODEL)
    return (y_prompt, y_sample, jnp.stack(gla_p), jnp.stack(conv_p), jnp.stack(gla_s), jnp.stack(conv_s))
```

```python
import functools

import jax
import jax.numpy as jnp
from jax import lax
from jax.experimental import pallas as pl
from jax.experimental.pallas import tpu as pltpu

F32 = jnp.float32
BF16 = jnp.bfloat16

D_MODEL = 1024
DEPTH = 4
CHUNK = 64
CONV_DIM = 512
CONV_W = 3
GLA_HEADS = 4
GLA_HEAD_K = 64
GLA_HEAD_V = 128
GLA_QK = GLA_HEADS * GLA_HEAD_K
GLA_DV = GLA_HEADS * GLA_HEAD_V
GATE_RANK = 16
GATE_TEMP = 16.0
D_FF = 3584
N_EXPERTS = 8
ALPHA = (2 * DEPTH) ** 0.25
LN_EPS = 1e-5

LANES = 128
SUBLANES = 8
VMEM_LIMIT_BYTES = 56 * 1024 * 1024

COL_CB, COL_CC, COL_CH = 0, CONV_DIM, 2 * CONV_DIM
COL_Q = 3 * CONV_DIM
COL_K = COL_Q + GLA_QK
COL_V = COL_K + GLA_QK
COL_G = COL_V + GLA_DV
COL_A = COL_G + GLA_DV
IN_COLS_PAD = COL_A + LANES

TB = 512
SUB = 256
CHUNKS_PER_BLOCK = TB // CHUNK
TF = 512
TM_MOE = 1024
TM_DENSE = 1024
MOE_CALLS = 4
ROUTE_ROWS = 8
ROW_I1, ROW_I2, ROW_G1, ROW_G2 = 0, 1, 2, 3


def _dot(a, b):
    return jnp.dot(a, b, preferred_element_type=F32)


def _dot_tn(a, b):
    return lax.dot_general(a, b, (((0,), (0,)), ((), ())), preferred_element_type=F32)


def _dot_nt(a, b):
    return lax.dot_general(a, b, (((1,), (1,)), ((), ())), preferred_element_type=F32)


def _split_bf16(a):
    hi = a.astype(BF16)
    lo = (a - hi.astype(F32)).astype(BF16)
    return hi, lo


def _parts(a, precise):
    return _split_bf16(a) if precise else (a.astype(BF16),)


def _mm(a, b, dot=_dot):
    r = dot(a[0], b[0])
    if len(a) > 1:
        r = r + dot(a[1], b[0])
    if len(b) > 1:
        r = r + dot(a[0], b[1])
    return r


def _load(refs, idx=None):
    return tuple(r[...] if idx is None else r[idx] for r in refs)


def _store(refs, idx, parts):
    for r, p in zip(refs, parts):
        r[idx] = p


def _layer_norm(r, g, b):
    mu = jnp.mean(r, axis=-1, keepdims=True)
    d = r - mu
    var = jnp.mean(d * d, axis=-1, keepdims=True)
    return d * lax.rsqrt(var + LN_EPS) * g + b


def _silu(g):
    return g / (1.0 + jnp.exp(-g))


def _mixer_kernel(*refs, sample, precise, blocks_per_seq, with_router, n_alias):
    np_ = 2 if precise else 1
    it = iter(refs)

    def take(n):
        return [next(it) for _ in range(n)]

    x_ref = next(it)
    if not sample:
        xn_ref = next(it)
    if sample:
        convin_ref = next(it)
        statein_ref = next(it)
    w_in_refs = take(np_)
    wa2_refs = take(np_)
    balpha_ref = next(it)
    convw_ref = next(it)
    gnorm_ref = next(it)
    w_out_refs = take(np_)
    lng_ref = next(it)
    lnb_ref = next(it)
    if with_router:
        router_ref = next(it)
    take(n_alias)
    x1_ref = next(it)
    if with_router:
        x1b_ref = next(it)
        route_ref = next(it)
    convout_ref = next(it)
    stateout_ref = next(it)
    proj_ref = next(it)
    ue_ref = next(it)
    prev_ref = next(it)
    fx_ref = next(it)
    y_ref = next(it)
    o_ref = next(it)
    qg_refs = take(np_)
    kd_refs = take(np_)
    v_refs = take(np_)
    la_refs = take(2)
    s_ref = next(it)
    qms_refs = take(np_)
    if not sample:
        xnb_ref = next(it)

    i = pl.program_id(0)
    seq_start = (i % blocks_per_seq) == 0

    def in_proj(src, cols):
        proj_ref[:, cols] = _mm(src, tuple(r[:, cols] for r in w_in_refs))

    if sample:
        in_proj(_parts(x_ref[...], precise), slice(None))
        project_ahead = lambda cols: None
    else:
        @pl.when(i == 0)
        def _():
            in_proj(_parts(x_ref[...], precise), slice(None))

        xnb_ref[...] = xn_ref[...].astype(BF16)
        project_ahead = lambda cols: in_proj((xnb_ref[...],), cols)

    if sample:
        prev_ref[6:8, :] = convin_ref[0]
    else:
        @pl.when(seq_start)
        def _():
            prev_ref[...] = jnp.zeros_like(prev_ref)

    w0 = convw_ref[0:1, :]
    w1 = convw_ref[1:2, :]
    w2 = convw_ref[2:3, :]
    u = proj_ref[:, COL_CC:COL_CC + CONV_DIM] * proj_ref[:, COL_CH:COL_CH + CONV_DIM]
    ue_ref[0:8, :] = prev_ref[...]
    ue_ref[8:8 + TB, :] = u
    conv = ue_ref[6:6 + TB, :] * w0 + ue_ref[7:7 + TB, :] * w1 + u * w2
    y_ref[:, 0:CONV_DIM] = proj_ref[:, COL_CB:COL_CB + CONV_DIM] * conv
    if sample:
        for c in range(CHUNKS_PER_BLOCK):
            r0 = c * CHUNK
            if c > 0:
                fx_ref[c, 6:8, :] = convin_ref[c]
                fx_ref[c, 8:16, :] = ue_ref[8 + r0:16 + r0, :]
                convc = fx_ref[c, 6:14, :] * w0 + fx_ref[c, 7:15, :] * w1 + fx_ref[c, 8:16, :] * w2
                y_ref[r0:r0 + 8, 0:CONV_DIM] = proj_ref[r0:r0 + 8, COL_CB:COL_CB + CONV_DIM] * convc
            convout_ref[c] = ue_ref[8 + r0 + CHUNK - 2:8 + r0 + CHUNK, :]
    else:
        prev_ref[...] = ue_ref[TB:TB + 8, :]
        convout_ref[0] = ue_ref[TB + 6:TB + 8, :]
    project_ahead(slice(COL_CB, COL_Q))

    z = _mm(_parts(proj_ref[:, COL_A:COL_A + LANES], precise), _load(wa2_refs)) + balpha_ref[...]
    la = (jnp.minimum(z, 0.0) - jnp.log(1.0 + jnp.exp(-jnp.abs(z)))) * (1.0 / GATE_TEMP)
    _store(la_refs, slice(None), _split_bf16(la))

    row = lax.broadcasted_iota(jnp.int32, (SUB, SUB), 0)
    col = lax.broadcasted_iota(jnp.int32, (SUB, SUB), 1)
    same_chunk = (row // CHUNK) == (col // CHUNK)
    causal = jnp.logical_and(same_chunk, row >= col)
    tril_bd = (causal.astype(BF16),)
    ones_bd = (same_chunk.astype(BF16),)
    lane_head = lax.broadcasted_iota(jnp.int32, (SUB, GLA_QK), 1) // GLA_HEAD_K

    for sb in range(TB // SUB):
        rs = slice(sb * SUB, (sb + 1) * SUB)
        lap = _load(la_refs, (rs, slice(None)))
        bcum = _mm(tril_bd, lap)
        blast = _mm(ones_bd, lap)
        q = proj_ref[rs, COL_Q:COL_Q + GLA_QK]
        k = proj_ref[rs, COL_K:COL_K + GLA_QK]
        qg = q * jnp.exp(bcum) * (GLA_HEAD_K ** -0.5)
        kg = _parts(k * jnp.exp(-bcum), precise)
        _store(qg_refs, (rs, slice(None)), _parts(qg, precise))
        _store(kd_refs, (rs, slice(None)), _parts(k * jnp.exp(blast - bcum), precise))
        vp = _parts(proj_ref[rs, COL_V:COL_V + GLA_DV], precise)
        _store(v_refs, (rs, slice(None)), vp)
        for h in range(GLA_HEADS):
            vs = slice(h * GLA_HEAD_V, (h + 1) * GLA_HEAD_V)
            qm = _parts(jnp.where(lane_head == h, qg, 0.0), precise)
            for cc in range(SUB // CHUNK):
                dst = (sb * (SUB // CHUNK) + cc, slice(h * CHUNK, (h + 1) * CHUNK), slice(None))
                _store(qms_refs, dst, tuple(p[cc * CHUNK:(cc + 1) * CHUNK] for p in qm))
            sc = jnp.where(causal, _mm(qm, kg, _dot_nt), 0.0)
            o_ref[rs, vs] = _mm(_parts(sc, precise), tuple(p[:, vs] for p in vp))

    project_ahead(slice(COL_Q, COL_G))
    project_ahead(slice(COL_A, IN_COLS_PAD))
    trow = lax.broadcasted_iota(jnp.int32, (TB, LANES), 0) // CHUNK
    tcol = lax.broadcasted_iota(jnp.int32, (TB, LANES), 1)
    chunk_sel = ((trow == tcol).astype(BF16),)
    dec_all = jnp.exp(_mm(_load(la_refs), chunk_sel, _dot_tn))
    if not sample:
        s = jnp.where(seq_start, 0.0, s_ref[...])
    for c in range(CHUNKS_PER_BLOCK):
        cs = slice(c * CHUNK, (c + 1) * CHUNK)
        if sample:
            s = statein_ref[c].reshape(GLA_QK, GLA_HEAD_V)
        oi = _mm(_load(qms_refs, c), _parts(s, precise))
        for h in range(GLA_HEADS):
            o_ref[cs, h * GLA_HEAD_V:(h + 1) * GLA_HEAD_V] += oi[h * CHUNK:(h + 1) * CHUNK, :]
        ds = _mm(_load(kd_refs, (cs, slice(None))), _load(v_refs, (cs, slice(None))), _dot_tn)
        ds_diag = jnp.concatenate([ds[h * GLA_HEAD_K:(h + 1) * GLA_HEAD_K, h * GLA_HEAD_V:(h + 1) * GLA_HEAD_V]
                                   for h in range(GLA_HEADS)], axis=0)
        s = dec_all[:, c:c + 1] * s + ds_diag
        if sample:
            stateout_ref[c] = s.reshape(GLA_HEADS, GLA_HEAD_K, GLA_HEAD_V)
    if not sample:
        s_ref[...] = s
        stateout_ref[0] = s.reshape(GLA_HEADS, GLA_HEAD_K, GLA_HEAD_V)

    gn = gnorm_ref[...]
    for h in range(GLA_HEADS):
        vs = slice(h * GLA_HEAD_V, (h + 1) * GLA_HEAD_V)
        oh = o_ref[:, vs]
        ms = jnp.mean(oh * oh, axis=-1, keepdims=True)
        g = proj_ref[:, COL_G + h * GLA_HEAD_V:COL_G + (h + 1) * GLA_HEAD_V]
        y_ref[:, CONV_DIM + h * GLA_HEAD_V:CONV_DIM + (h + 1) * GLA_HEAD_V] = (
            oh * lax.rsqrt(ms + LN_EPS) * gn * _silu(g))

    project_ahead(slice(COL_G, COL_A))

    m = _mm(_parts(y_ref[...], precise), _load(w_out_refs))
    x1 = _layer_norm(ALPHA * x_ref[...] + m, lng_ref[...], lnb_ref[...])
    x1_ref[...] = x1

    if with_router:
        x1b_ref[...] = x1.astype(BF16)
        x1h, x1l = _split_bf16(x1)
        lg = _dot(x1h, router_ref[...]) + _dot(x1l, router_ref[...])
        lg = lg[:, 0:LANES] + lg[:, LANES:2 * LANES]
        lg = lg.T[0:N_EXPERTS, :]
        eid = lax.broadcasted_iota(jnp.int32, lg.shape, 0)
        neg = jnp.float32(-jnp.inf)
        m1 = jnp.max(lg, axis=0, keepdims=True)
        i1 = jnp.min(jnp.where(lg == m1, eid, N_EXPERTS), axis=0, keepdims=True)
        lg2 = jnp.where(eid == i1, neg, lg)
        m2 = jnp.max(lg2, axis=0, keepdims=True)
        i2 = jnp.min(jnp.where(lg2 == m2, eid, N_EXPERTS), axis=0, keepdims=True)
        e2 = jnp.exp(m2 - m1)
        g1 = 1.0 / (1.0 + e2)
        g2 = e2 / (1.0 + e2)
        rid = lax.broadcasted_iota(jnp.int32, (ROUTE_ROWS, TB), 0)
        slab = jnp.where(rid == ROW_I1, i1.astype(F32), 0.0)
        slab = jnp.where(rid == ROW_I2, i2.astype(F32), slab)
        slab = jnp.where(rid == ROW_G1, g1, slab)
        slab = jnp.where(rid == ROW_G2, g2, slab)
        route_ref[...] = slab


def _mixer_call(x, w, *, n_rows, n_blocks, blocks_per_seq, with_router, sample_state=None, into=None):
    sample = sample_state is not None
    precise = sample
    np_ = 2 if precise else 1
    const2 = lambda i: (0, 0)
    const3 = lambda i: (0, 0, 0)
    const4 = lambda i: (0, 0, 0, 0)

    x_last = x.shape[0] // TB - 1
    in_specs = [pl.BlockSpec((TB, D_MODEL), (lambda i: (x_last, 0)) if sample else (lambda i: (i, 0)))]
    args = [x]
    if not sample:
        in_specs.append(pl.BlockSpec((TB, D_MODEL), lambda i: (jnp.minimum(i + 1, n_blocks - 1), 0)))
        args.append(x)
    if sample:
        conv_in, state_in = sample_state
        in_specs += [pl.BlockSpec(conv_in.shape, const3), pl.BlockSpec(state_in.shape, const4)]
        args += [conv_in, state_in]
    in_specs += [pl.BlockSpec((D_MODEL, IN_COLS_PAD), const2)] * np_
    args += list(w["w_in"][:np_])
    in_specs += [pl.BlockSpec((LANES, GLA_QK), const2)] * np_
    args += list(w["wa2"][:np_])
    in_specs += [pl.BlockSpec((1, GLA_QK), const2), pl.BlockSpec((CONV_W, CONV_DIM), const2),
                 pl.BlockSpec((1, GLA_HEAD_V), const2)]
    args += [w["b_alpha"], w["conv_w"], w["gla_norm_g"]]
    in_specs += [pl.BlockSpec((D_MODEL, D_MODEL), const2)] * np_
    args += list(w["w_out"][:np_])
    in_specs += [pl.BlockSpec((1, D_MODEL), const2)] * 2
    args += [w["ln_g"], w["ln_b"]]
    if with_router:
        in_specs += [pl.BlockSpec((D_MODEL, 2 * LANES), const2)]
        args += [w["router"]]
    aliases = {}
    n_alias = 0
    if sample:
        n_alias = len(into)
        for k, buf in enumerate(into):
            aliases[len(args)] = k
            in_specs.append(pl.BlockSpec(memory_space=pl.ANY))
            args.append(buf)

    out_row = (lambda i: (n_blocks, 0)) if sample else (lambda i: (i, 0))
    out_shape = [jax.ShapeDtypeStruct((n_rows, D_MODEL), F32)]
    out_specs = [pl.BlockSpec((TB, D_MODEL), out_row)]
    if with_router:
        out_shape += [jax.ShapeDtypeStruct((n_rows, D_MODEL), BF16),
                      jax.ShapeDtypeStruct((ROUTE_ROWS, n_rows), F32)]
        out_specs += [pl.BlockSpec((TB, D_MODEL), out_row),
                      pl.BlockSpec((ROUTE_ROWS, TB), (lambda i: (0, n_blocks)) if sample else (lambda i: (0, i)))]
    if sample:
        out_shape += [jax.ShapeDtypeStruct(conv_in.shape, F32), jax.ShapeDtypeStruct(state_in.shape, F32)]
        out_specs += [pl.BlockSpec(conv_in.shape, const3), pl.BlockSpec(state_in.shape, const4)]
    else:
        n_seq = n_blocks // blocks_per_seq
        out_shape += [jax.ShapeDtypeStruct((n_seq, CONV_W - 1, CONV_DIM), F32),
                      jax.ShapeDtypeStruct((n_seq, GLA_HEADS, GLA_HEAD_K, GLA_HEAD_V), F32)]
        out_specs += [pl.BlockSpec((1, CONV_W - 1, CONV_DIM), lambda i: (i // blocks_per_seq, 0, 0)),
                      pl.BlockSpec((1, GLA_HEADS, GLA_HEAD_K, GLA_HEAD_V),
                                   lambda i: (i // blocks_per_seq, 0, 0, 0))]
    scratch = [
        pltpu.VMEM((TB, IN_COLS_PAD), F32),
        pltpu.VMEM((TB + 8, CONV_DIM), F32),
        pltpu.VMEM((8, CONV_DIM), F32),
        pltpu.VMEM((CHUNKS_PER_BLOCK, 16, CONV_DIM), F32),
        pltpu.VMEM((TB, D_MODEL), F32),
        pltpu.VMEM((TB, GLA_DV), F32),
    ]
    scratch += [pltpu.VMEM((TB, GLA_QK), BF16)] * np_
    scratch += [pltpu.VMEM((TB, GLA_QK), BF16)] * np_
    scratch += [pltpu.VMEM((TB, GLA_DV), BF16)] * np_
    scratch += [pltpu.VMEM((TB, GLA_QK), BF16)] * 2
    scratch += [pltpu.VMEM((GLA_QK, GLA_HEAD_V), F32)]
    scratch += [pltpu.VMEM((CHUNKS_PER_BLOCK, GLA_HEADS * CHUNK, GLA_QK), BF16)] * np_
    if not sample:
        scratch += [pltpu.VMEM((TB, D_MODEL), BF16)]
    kern = functools.partial(_mixer_kernel, sample=sample, precise=precise, blocks_per_seq=blocks_per_seq,
                             with_router=with_router, n_alias=n_alias)
    name = ("mixer_sample" if sample else "mixer_prompt") + ("_router" if with_router else "")
    return pl.pallas_call(
        kern,
        grid=(1 if sample else n_blocks,),
        in_specs=in_specs,
        out_specs=out_specs,
        out_shape=out_shape,
        scratch_shapes=scratch,
        input_output_aliases=aliases,
        compiler_params=pltpu.CompilerParams(dimension_semantics=("arbitrary",),
                                             vmem_limit_bytes=VMEM_LIMIT_BYTES),
        name=name,
    )(*args)


def _swiglu_step(xp, wg, wu, wd, acc_ref, precise):
    f = pl.program_id(1)
    g = _mm(xp, _parts(wg, precise))
    u = _mm(xp, _parts(wu, precise))
    part = _mm(_parts(_silu(g) * u, precise), _parts(wd, precise))

    @pl.when(f == 0)
    def _():
        acc_ref[...] = part

    @pl.when(f > 0)
    def _():
        acc_ref[...] += part


def _ffn_dense_kernel(*refs, precise, n_alias):
    x_ref, wg_ref, wu_ref, wd_ref, lng_ref, lnb_ref = refs[:6]
    out_ref, acc_ref = refs[6 + n_alias:]
    x = x_ref[...]
    _swiglu_step(_parts(x, precise), wg_ref[0], wu_ref[0], wd_ref[0], acc_ref, precise)

    @pl.when(pl.program_id(1) == pl.num_programs(1) - 1)
    def _():
        out_ref[...] = _layer_norm(ALPHA * x + acc_ref[...], lng_ref[...], lnb_ref[...])


def _ffn_dense_call(x, wg, wu, wd, e, ln_g, ln_b, *, tm, first_block, n_blocks, precise, into=None):
    n_rows = x.shape[0]
    nf = D_FF // TF
    row_spec = pl.BlockSpec((tm, D_MODEL), lambda t, f: (first_block + t, 0))
    in_specs = [row_spec,
                pl.BlockSpec((1, D_MODEL, TF), lambda t, f: (e, 0, f)),
                pl.BlockSpec((1, D_MODEL, TF), lambda t, f: (e, 0, f)),
                pl.BlockSpec((1, TF, D_MODEL), lambda t, f: (e, f, 0)),
                pl.BlockSpec((1, D_MODEL), lambda t, f: (0, 0)),
                pl.BlockSpec((1, D_MODEL), lambda t, f: (0, 0))]
    args = [x, wg, wu, wd, ln_g, ln_b]
    aliases = {}
    n_alias = 0
    if into is not None:
        n_alias = 1
        aliases[len(args)] = 0
        in_specs.append(pl.BlockSpec(memory_space=pl.ANY))
        args.append(into)
    return pl.pallas_call(
        functools.partial(_ffn_dense_kernel, precise=precise, n_alias=n_alias),
        grid=(n_blocks, nf),
        in_specs=in_specs,
        out_specs=row_spec,
        out_shape=jax.ShapeDtypeStruct((n_rows, D_MODEL), F32),
        scratch_shapes=[pltpu.VMEM((tm, D_MODEL), F32)],
        input_output_aliases=aliases,
        compiler_params=pltpu.CompilerParams(dimension_semantics=("arbitrary", "arbitrary"),
                                             vmem_limit_bytes=VMEM_LIMIT_BYTES),
        name="ffn_dense_precise" if precise else "ffn_dense",
    )(*args)


def _ffn_moe_kernel(te_ref, nt_ref, x_ref, wg_ref, wu_ref, wd_ref, *rest, first_tile, n_alias):
    out_ref, acc_ref = rest[n_alias:]
    used = first_tile + pl.program_id(0) < nt_ref[0]

    @pl.when(used)
    def _():
        _swiglu_step((x_ref[...],), wg_ref[0, 0], wu_ref[0, 0], wd_ref[0, 0], acc_ref, False)

    last = pl.program_id(1) == pl.num_programs(1) - 1

    @pl.when(jnp.logical_and(last, used))
    def _():
        out_ref[...] = acc_ref[...]

    @pl.when(jnp.logical_and(last, jnp.logical_not(used)))
    def _():
        out_ref[...] = jnp.zeros_like(out_ref)


def _ffn_moe_call(tile_expert, n_tiles_used, xs, wg, wu, wd, e, *, first_tile, n_sorted, into=None):
    nf = D_FF // TF
    in_specs = [
        pl.BlockSpec((TM_MOE, D_MODEL), lambda t, f, te, nt: (t, 0)),
        pl.BlockSpec((1, 1, D_MODEL, TF), lambda t, f, te, nt: (e, te[first_tile + t], 0, f)),
        pl.BlockSpec((1, 1, D_MODEL, TF), lambda t, f, te, nt: (e, te[first_tile + t], 0, f)),
        pl.BlockSpec((1, 1, TF, D_MODEL), lambda t, f, te, nt: (e, te[first_tile + t], f, 0)),
    ]
    args = [tile_expert, n_tiles_used, xs, wg, wu, wd]
    aliases = {}
    n_alias = 0
    if into is not None:
        n_alias = 1
        aliases[len(args)] = 0
        in_specs.append(pl.BlockSpec(memory_space=pl.ANY))
        args.append(into)
    grid_spec = pltpu.PrefetchScalarGridSpec(
        num_scalar_prefetch=2,
        grid=(xs.shape[0] // TM_MOE, nf),
        in_specs=in_specs,
        out_specs=pl.BlockSpec((TM_MOE, D_MODEL), lambda t, f, te, nt: (first_tile + t, 0)),
        scratch_shapes=[pltpu.VMEM((TM_MOE, D_MODEL), F32)],
    )
    return pl.pallas_call(
        functools.partial(_ffn_moe_kernel, first_tile=first_tile, n_alias=n_alias),
        grid_spec=grid_spec,
        out_shape=jax.ShapeDtypeStruct((n_sorted, D_MODEL), F32),
        input_output_aliases=aliases,
        compiler_params=pltpu.CompilerParams(dimension_semantics=("arbitrary", "arbitrary"),
                                             vmem_limit_bytes=VMEM_LIMIT_BYTES),
        name="ffn_moe",
    )(*args)


def _combine_kernel(x_ref, ya_ref, yb_ref, route_ref, lng_ref, lnb_ref, out_ref):
    pad = jnp.zeros((LANES - ROUTE_ROWS, TB), F32)
    cols = jnp.concatenate([route_ref[...], pad], axis=0).T
    g1 = cols[:, ROW_G1:ROW_G1 + 1]
    g2 = cols[:, ROW_G2:ROW_G2 + 1]
    r = ALPHA * x_ref[...] + (g1 * ya_ref[...] + g2 * yb_ref[...])
    out_ref[...] = _layer_norm(r, lng_ref[...], lnb_ref[...])


def _combine_call(x, ya, yb, route, ln_g, ln_b, *, first_block, n_blocks):
    in_spec = pl.BlockSpec((TB, D_MODEL), lambda t: (first_block + t, 0))
    route_spec = pl.BlockSpec((ROUTE_ROWS, TB), lambda t: (0, first_block + t))
    vec_spec = pl.BlockSpec((1, D_MODEL), lambda t: (0, 0))
    return pl.pallas_call(
        _combine_kernel,
        grid=(n_blocks,),
        in_specs=[in_spec, in_spec, in_spec, route_spec, vec_spec, vec_spec],
        out_specs=pl.BlockSpec((TB, D_MODEL), lambda t: (t, 0)),
        out_shape=jax.ShapeDtypeStruct((n_blocks * TB, D_MODEL), F32),
        compiler_params=pltpu.CompilerParams(dimension_semantics=("arbitrary",),
                                             vmem_limit_bytes=VMEM_LIMIT_BYTES,
                                             allow_input_fusion=[False, True, True, False, False, False]),
        name="moe_combine",
    )(x, ya, yb, route, ln_g, ln_b)


def _take_rows(a, idx):
    return a.at[idx].get(mode="promise_in_bounds")


def _moe_dispatch(route, n_rows):
    i1 = route[ROW_I1].astype(jnp.int32)
    i2 = route[ROW_I2].astype(jnp.int32)
    experts = jnp.arange(N_EXPERTS, dtype=jnp.int32)
    hit = ((i1[:, None] == experts) | (i2[:, None] == experts)).astype(jnp.int32)
    before = jnp.cumsum(hit, axis=0) - hit
    counts = jnp.sum(hit, axis=0)
    padded = ((counts + TM_MOE - 1) // TM_MOE) * TM_MOE
    ends = jnp.cumsum(padded)
    starts = ends - padded
    rank1 = jnp.sum(jnp.where(i1[:, None] == experts, before, 0), axis=1)
    rank2 = jnp.sum(jnp.where(i2[:, None] == experts, before, 0), axis=1)
    p1 = _take_rows(starts, i1) + rank1
    p2 = _take_rows(starts, i2) + rank2
    n_tiles = (2 * n_rows + N_EXPERTS * (TM_MOE - 1) + TM_MOE - 1) // TM_MOE
    tile_start = jnp.arange(n_tiles, dtype=jnp.int32) * TM_MOE
    tile_expert = jnp.minimum(jnp.sum(tile_start[:, None] >= ends[None, :], axis=1), N_EXPERTS - 1)
    tile_expert = tile_expert.astype(jnp.int32)
    n_tiles_used = (ends[-1] // TM_MOE).reshape(1)
    tok = jnp.arange(n_rows, dtype=jnp.int32)
    _, by_pos = lax.sort((jnp.concatenate([p1, p2]), jnp.concatenate([tok, tok])), num_keys=1)
    pad_before = jnp.repeat(_take_rows(starts - (jnp.cumsum(counts) - counts), tile_expert), TM_MOE)
    pos = jnp.arange(n_tiles * TM_MOE, dtype=jnp.int32)
    row_src = _take_rows(by_pos, jnp.clip(pos - pad_before, 0, 2 * n_rows - 1))
    return row_src, p1, p2, tile_expert, n_tiles_used.astype(jnp.int32)


def _split_kernel(a_ref, hi_ref, lo_ref):
    hi, lo = _split_bf16(a_ref[...])
    hi_ref[...] = hi
    lo_ref[...] = lo


def _split_call(a):
    rows, cols = a.shape
    tr = min(rows, 256)
    spec = pl.BlockSpec((tr, cols), lambda r: (r, 0))
    return pl.pallas_call(
        _split_kernel,
        grid=(rows // tr,),
        in_specs=[spec],
        out_specs=[spec, spec],
        out_shape=[jax.ShapeDtypeStruct(a.shape, BF16)] * 2,
        name="split_bf16",
    )(a)


def _pad_w_in(w_in):
    main = w_in[:, :COL_A]
    alr = jnp.pad(w_in[:, COL_A:], ((0, 0), (0, LANES - GATE_RANK)))
    return jnp.concatenate([main, alr], axis=1)


def kernel(x_prompt, x_sample, cache_conv, state_gla, ln_mix_g, ln_mix_b, w_in, w_alpha2, b_alpha,
           conv_w, gla_norm_g, w_out, ln_ffn_g, ln_ffn_b, ffn_w_gate, ffn_w_up, ffn_w_down,
           moe_router, moe_w_gate, moe_w_up, moe_w_down):
    n_prompt_seq, seq_len, _ = x_prompt.shape
    n_sample, dec_len, _ = x_sample.shape
    assert dec_len == CHUNK and n_sample * dec_len == TB and seq_len % TB == 0
    n_prompt_rows = n_prompt_seq * seq_len
    n_rows = n_prompt_rows + TB
    blocks_per_seq = seq_len // TB
    n_prompt_blocks = n_prompt_rows // TB
    assert n_prompt_rows % TM_DENSE == 0

    xp = x_prompt.reshape(n_prompt_rows, D_MODEL)
    xs = x_sample.reshape(TB, D_MODEL)
    gla_p, conv_p, gla_s, conv_s = [], [], [], []
    for l in range(DEPTH):
        moe_layer = l % 2 == 1
        e = l // 2
        w = {
            "w_in": _split_call(_pad_w_in(w_in[l])),
            "wa2": _split_call(jnp.pad(w_alpha2[l], ((0, LANES - GATE_RANK), (0, 0)))),
            "b_alpha": b_alpha[l][None, :],
            "conv_w": conv_w[l],
            "gla_norm_g": gla_norm_g[l][None, :],
            "w_out": _split_call(w_out[l]),
            "ln_g": ln_mix_g[l][None, :],
            "ln_b": ln_mix_b[l][None, :],
        }
        if moe_layer:
            w["router"] = jnp.concatenate(
                _split_call(jnp.pad(moe_router[e], ((0, 0), (0, LANES - N_EXPERTS)))), axis=1)
        common = dict(n_rows=n_rows, n_blocks=n_prompt_blocks, blocks_per_seq=blocks_per_seq,
                      with_router=moe_layer)
        outs = _mixer_call(xp, w, **common)
        n_stream = 3 if moe_layer else 1
        cp, sp = outs[n_stream:]
        outs_s = _mixer_call(xs, w, sample_state=(cache_conv[l], state_gla[l]), into=outs[:n_stream], **common)
        cs, ss = outs_s[n_stream:]
        conv_p.append(cp)
        gla_p.append(sp)
        conv_s.append(cs)
        gla_s.append(ss)

        ln_g = ln_ffn_g[l][None, :]
        ln_b = ln_ffn_b[l][None, :]
        if moe_layer:
            x1, x1b, route = outs_s[:3]
            row_src, p1, p2, tile_expert, n_tiles_used = _moe_dispatch(route, n_rows)
            n_tiles = row_src.shape[0] // TM_MOE
            bounds = [n_tiles * c // MOE_CALLS for c in range(MOE_CALLS + 1)]
            ys = None
            for t0, t1 in zip(bounds[:-1], bounds[1:]):
                xsort = _take_rows(x1b, row_src[t0 * TM_MOE:t1 * TM_MOE])
                ys = _ffn_moe_call(tile_expert, n_tiles_used, xsort, moe_w_gate, moe_w_up, moe_w_down, e,
                                   first_tile=t0, n_sorted=row_src.shape[0], into=ys)
            comb = functools.partial(_combine_call, x1, _take_rows(ys, p1), _take_rows(ys, p2), route, ln_g, ln_b)
            if l == DEPTH - 1:
                xp = comb(first_block=0, n_blocks=n_prompt_blocks)
                xs = comb(first_block=n_prompt_blocks, n_blocks=1)
            else:
                xp = xs = comb(first_block=0, n_blocks=n_prompt_blocks + 1)
        else:
            x1 = outs_s[0]
            dense = functools.partial(_ffn_dense_call, x1, ffn_w_gate, ffn_w_up, ffn_w_down, e, ln_g, ln_b)
            x2 = dense(tm=TM_DENSE, first_block=0, n_blocks=n_prompt_rows // TM_DENSE, precise=False)
            xp = xs = dense(tm=TB, first_block=n_prompt_blocks, n_blocks=1, precise=(l == 0), into=x2)

    y_prompt = xp[:n_prompt_rows].reshape(n_prompt_seq, seq_len, D_MODEL)
    y_sample = xs[xs.shape[0] - TB:].reshape(n_sample, dec_len, D_MODEL)
    return (y_prompt, y_sample, jnp.stack(gla_p), jnp.stack(conv_p), jnp.stack(gla_s), jnp.stack(conv_s))
```
